```python
import jax, jax.numpy as jnp
from jax import lax
import numpy as np

D_MODEL = 1024
BATCH = 8
SEQ = 4096
DEPTH = 4

ATT_HEADS = 8
ATT_HEAD_DIM = 64
ATT_ROPE_DIM = ATT_HEAD_DIM // 4
ATT_ROPE_THETA = 500000.0
DILATED_GROUPS = ((128, 1), (512, 4), (2048, 16))
RET_HEADS = 8
RET_QK_DIM = 64
RET_V_DIM = 2 * RET_QK_DIM
RET_THETA = 10000.0
RET_CHUNK = 128
D_FF = 4 * D_MODEL
PLE_DIM = 256
NORM_EPS = 1e-6
GN_EPS = 1e-5
MASK_VALUE = -1e30

ATT_W = ATT_HEADS * ATT_HEAD_DIM
RET_QK_W = RET_HEADS * RET_QK_DIM
RET_V_W = RET_HEADS * RET_V_DIM
IN_SPLITS = (ATT_W, ATT_W, ATT_W, RET_QK_W, RET_QK_W, RET_V_W, RET_V_W, D_MODEL, D_MODEL)
IN_WIDTH = sum(IN_SPLITS)

kernel_name = "hybrid_dilated_attn_retention_block"


def rms_norm(x, gain=None):
    xf = x.astype(jnp.float32)
    y = xf * lax.rsqrt(jnp.mean(xf * xf, axis=-1, keepdims=True) + NORM_EPS)
    if gain is not None:
        y = y * gain.astype(jnp.float32)
    return y.astype(x.dtype)


def rotate(x, pos, rot_dim, theta):
    half = rot_dim // 2
    inv = theta ** (-jnp.arange(half, dtype=jnp.float32) * 2.0 / rot_dim)
    ang = pos.astype(jnp.float32)[:, None] * inv[None, :]
    cos = jnp.cos(ang)[:, None, :]
    sin = jnp.sin(ang)[:, None, :]
    xr = x[..., :rot_dim].astype(jnp.float32)
    x1, x2 = xr[..., :half], xr[..., half:]
    rot = jnp.concatenate([x1 * cos - x2 * sin, x2 * cos + x1 * sin], axis=-1).astype(x.dtype)
    return jnp.concatenate([rot, x[..., rot_dim:]], axis=-1)


def dilated_band_attention(q, k, v, dilation, half_width):
    b, s, h, dh = q.shape
    r = dilation
    L = s // r
    blk = half_width
    nb = -(-L // blk)
    Lp = nb * blk

    def to_sub(t):
        return t.reshape(b, L, r, h, dh).transpose(0, 2, 3, 1, 4)

    qs = jnp.pad(to_sub(q), ((0, 0), (0, 0), (0, 0), (0, Lp - L), (0, 0))).reshape(b, r, h, nb, blk, dh)

    def windows(t):
        t = jnp.pad(to_sub(t), ((0, 0), (0, 0), (0, 0), (blk, Lp - L + blk), (0, 0))).reshape(b, r, h, nb + 2, blk, dh)
        return jnp.concatenate([t[:, :, :, :-2], t[:, :, :, 1:-1], t[:, :, :, 2:]], axis=4)

    kw, vw = windows(k), windows(v)
    scores = jnp.einsum('brhnqd,brhnkd->brhnqk', qs.astype(jnp.float32), kw.astype(jnp.float32)) * (dh ** -0.5)
    qi = jnp.arange(blk)
    kj = jnp.arange(3 * blk)
    rel = qi[:, None] - kj[None, :] + blk
    key_idx = jnp.arange(nb)[:, None] * blk + kj[None, :] - blk
    valid = ((jnp.abs(rel) <= half_width)[None]
             & ((key_idx >= 0) & (key_idx < L))[:, None, :])
    scores = jnp.where(valid, scores, MASK_VALUE)
    lse = jax.nn.logsumexp(scores, axis=-1)
    probs = jnp.exp(scores - lse[..., None])
    out = jnp.einsum('brhnqk,brhnkd->brhnqd', probs, vw.astype(jnp.float32))
    out = out.reshape(b, r, h, Lp, dh)[:, :, :, :L].transpose(0, 3, 1, 2, 4).reshape(b, s, h, dh)
    lse = lse.reshape(b, r, h, Lp)[..., :L].transpose(0, 3, 1, 2).reshape(b, s, h)
    return out, lse


def dilated_attention_mixer(q, k, v):
    b, s, _ = q.shape
    pos = jnp.arange(s)
    q = rotate(q.reshape(b, s, ATT_HEADS, ATT_HEAD_DIM), pos, ATT_ROPE_DIM, ATT_ROPE_THETA)
    k = rotate(k.reshape(b, s, ATT_HEADS, ATT_HEAD_DIM), pos, ATT_ROPE_DIM, ATT_ROPE_THETA)
    v = v.reshape(b, s, ATT_HEADS, ATT_HEAD_DIM)
    outs, lses = [], []
    for window, dil in DILATED_GROUPS:
        o, l = dilated_band_attention(q, k, v, dil, window // (2 * dil))
        outs.append(o)
        lses.append(l)
    weights = jax.nn.softmax(jnp.stack(lses, axis=0), axis=0)
    out = jnp.sum(weights[..., None] * jnp.stack(outs, axis=0), axis=0)
    return out.reshape(b, s, ATT_W).astype(v.dtype)


def retention_direction(q, k, v, log_gamma, include_diag):
    b, h, s, dk = q.shape
    dv = v.shape[-1]
    c = RET_CHUNK
    n = s // c
    qc = q.reshape(b, h, n, c, dk)
    kc = k.reshape(b, h, n, c, dk)
    vc = v.reshape(b, h, n, c, dv)
    idx = jnp.arange(c, dtype=jnp.float32)
    diff = idx[:, None] - idx[None, :]
    inside = (diff >= 0) if include_diag else (diff > 0)
    lg = log_gamma[:, None, None]
    decay = jnp.where(inside[None], jnp.exp(jnp.maximum(diff, 0.0)[None] * lg), 0.0)
    intra = jnp.einsum('bhnid,bhnjd->bhnij', qc, kc) * decay[None, :, None]
    intra = jnp.einsum('bhnij,bhnje->bhnie', intra, vc)
    zeta = jnp.exp((c - 1 - idx)[None, :] * log_gamma[:, None])
    xi = jnp.exp((idx + 1)[None, :] * log_gamma[:, None])
    kv = jnp.einsum('bhnjd,bhnje->nbhde', kc * zeta[None, :, None, :, None], vc)
    chunk_decay = jnp.exp(c * log_gamma)[None, :, None, None]

    def step(state, kv_n):
        return chunk_decay * state + kv_n, state

    _, prev = lax.scan(step, jnp.zeros((b, h, dk, dv), jnp.float32), kv)
    cross = jnp.einsum('bhnid,nbhde->bhnie', qc * xi[None, :, None, :, None], prev)
    return (intra + cross).reshape(b, h, s, dv)


def retention_mixer(q, k, v, g, decay_logit):
    b, s, _ = q.shape
    pos = jnp.arange(s)
    q = rotate(q.reshape(b, s, RET_HEADS, RET_QK_DIM), pos, RET_QK_DIM, RET_THETA)
    k = rotate(k.reshape(b, s, RET_HEADS, RET_QK_DIM), pos, RET_QK_DIM, RET_THETA)
    q = q.astype(jnp.float32).transpose(0, 2, 1, 3)
    k = k.astype(jnp.float32).transpose(0, 2, 1, 3) * (RET_QK_DIM ** -0.5)
    v = v.reshape(b, s, RET_HEADS, RET_V_DIM).astype(jnp.float32).transpose(0, 2, 1, 3)
    log_gamma = jax.nn.log_sigmoid(decay_logit.astype(jnp.float32))
    fwd = retention_direction(q, k, v, log_gamma[0], True)
    bwd = retention_direction(jnp.flip(q, 2), jnp.flip(k, 2), jnp.flip(v, 2), log_gamma[1], False)
    y = fwd + jnp.flip(bwd, 2)
    mu = jnp.mean(y, axis=-1, keepdims=True)
    var = jnp.mean(jnp.square(y - mu), axis=-1, keepdims=True)
    y = (y - mu) * lax.rsqrt(var + GN_EPS)
    y = y.transpose(0, 2, 1, 3).reshape(b, s, RET_V_W)
    return (jax.nn.silu(g.astype(jnp.float32)) * y).astype(g.dtype)


def setup_inputs(seed: int = 0) -> dict:
    key = jax.random.key(seed)
    ks = jax.random.split(key, 16)

    def dense(k, shape, fan_in):
        return jax.random.normal(k, shape, jnp.float32) * (fan_in ** -0.5)

    def gain(k):
        return 1.0 + 0.05 * jax.random.normal(k, (DEPTH, D_MODEL), jnp.float32)

    expo = 5.0 + jnp.arange(RET_HEADS, dtype=jnp.float32)
    base_logit = jnp.log(jnp.exp2(expo) - 1.0)
    ret_decay_logit = base_logit[None, None, :] + 0.1 * jax.random.normal(ks[10], (DEPTH, 2, RET_HEADS), jnp.float32)
    return {
        "x": jax.random.normal(ks[0], (BATCH, SEQ, D_MODEL), jnp.float32),
        "p": jax.random.normal(ks[1], (DEPTH, BATCH, SEQ, PLE_DIM), jnp.float32),
        "w_in": dense(ks[2], (DEPTH, D_MODEL, IN_WIDTH), D_MODEL),
        "w_att_out": dense(ks[3], (DEPTH, ATT_W, D_MODEL), ATT_W),
        "w_ret_out": dense(ks[4], (DEPTH, RET_V_W, D_MODEL), RET_V_W),
        "w_out": dense(ks[5], (DEPTH, D_MODEL, D_MODEL), D_MODEL),
        "w_mlp_up": dense(ks[6], (DEPTH, D_MODEL, D_FF), D_MODEL),
        "w_mlp_down": dense(ks[7], (DEPTH, D_FF, D_MODEL), D_FF),
        "w_ple_gate": dense(ks[8], (DEPTH, D_MODEL, D_MODEL), D_MODEL),
        "w_ple_proj": dense(ks[9], (DEPTH, PLE_DIM, D_MODEL), PLE_DIM),
        "ret_decay_logit": ret_decay_logit,
        "norm_mix_pre": gain(ks[11]),
        "norm_mix_post": gain(ks[12]),
        "norm_mlp_pre": gain(ks[13]),
        "norm_mlp_post": gain(ks[14]),
        "norm_ple": gain(ks[15]),
    }


def reference(x, p, w_in, w_att_out, w_ret_out, w_out, w_mlp_up, w_mlp_down, w_ple_gate, w_ple_proj,
              ret_decay_logit, norm_mix_pre, norm_mix_post, norm_mlp_pre, norm_mlp_post, norm_ple):
    splits = np.cumsum(IN_SPLITS)[:-1].tolist()
    h = x
    for i in range(DEPTH):
        u = rms_norm(h, norm_mix_pre[i])
        proj = u @ w_in[i]
        qa, ka, va, qr, kr, vr, gr, gate_a, gate_b = jnp.split(proj, splits, axis=-1)
        att = dilated_attention_mixer(qa, ka, va)
        ret = retention_mixer(qr, kr, vr, gr, ret_decay_logit[i])
        merged = (jax.nn.sigmoid(gate_a) * (att @ w_att_out[i])
                  + jax.nn.sigmoid(gate_b) * (ret @ w_ret_out[i]))
        h = h + rms_norm(merged @ w_out[i], norm_mix_post[i])
        u = rms_norm(h, norm_mlp_pre[i])
        ff = jnp.square(jax.nn.relu(u @ w_mlp_up[i])) @ w_mlp_down[i]
        h = h + rms_norm(ff, norm_mlp_post[i])
        gate = jax.nn.sigmoid(rms_norm(h) @ w_ple_gate[i])
        h = h + gate * rms_norm(p[i] @ w_ple_proj[i], norm_ple[i])
    return h
```

```python
import functools

import jax
import jax.numpy as jnp
from jax import lax
from jax.experimental import pallas as pl
from jax.experimental.pallas import tpu as pltpu

D_MODEL = 1024
ATT_HEADS = 8
ATT_HEAD_DIM = 64
ATT_ROPE_DIM = 16
ATT_ROPE_THETA = 500000.0
DILATIONS = (1, 4, 16)
HALF_WIDTH = 64
RET_HEADS = 8
RET_QK_DIM = 64
RET_V_DIM = 128
RET_THETA = 10000.0
RET_CHUNK = 128
D_FF = 4096
PLE_DIM = 256
NORM_EPS = 1e-6
GN_EPS = 1e-5
MASK_VALUE = -1e30

ATT_W = ATT_HEADS * ATT_HEAD_DIM
RET_QK_W = RET_HEADS * RET_QK_DIM
RET_V_W = RET_HEADS * RET_V_DIM
IN_WIDTH = 3 * ATT_W + 2 * RET_QK_W + 2 * RET_V_W + 2 * D_MODEL

LANES = 128
QBLK = 128
KBLK = QBLK + 2 * HALF_WIDTH
VMEM_LIMIT = 56 * 1024 * 1024

F32 = jnp.float32
BF16 = jnp.bfloat16


def _rms(x):
    return x * lax.rsqrt(jnp.mean(x * x, axis=-1, keepdims=True) + NORM_EPS)


def _sigmoid(x):
    return 1.0 / (1.0 + jnp.exp(-x))


def _rotate(x, cos, sin_up, sin_dn, shift):
    return (x * cos + pltpu.roll(x, LANES - shift, 1) * sin_up
            + pltpu.roll(x, shift, 1) * sin_dn)


IN_TM = 512
IN_TN = 512
_IN_OUT_WIDTHS = (3 * ATT_W, 2 * RET_QK_W, RET_V_W, RET_V_W, D_MODEL, D_MODEL)


def _in_proj_kernel(x_ref, g_ref, w_ref, *rest):
    out_refs, u_ref = rest[:-1], rest[-1]
    u_ref[...] = (_rms(x_ref[...]) * g_ref[...]).astype(BF16)
    off = 0
    for o_ref, width in zip(out_refs, _IN_OUT_WIDTHS):
        for c in range(0, width, IN_TN):
            o_ref[:, c:c + IN_TN] = jnp.dot(
                u_ref[...], w_ref[:, off + c:off + c + IN_TN],
                preferred_element_type=F32).astype(BF16)
        off += width


def _in_proj(h, gain, w):
    t = h.shape[0]
    return pl.pallas_call(
        _in_proj_kernel,
        grid=(t // IN_TM,),
        in_specs=[
            pl.BlockSpec((IN_TM, D_MODEL), lambda i: (i, 0)),
            pl.BlockSpec((1, D_MODEL), lambda i: (0, 0)),
            pl.BlockSpec((D_MODEL, IN_WIDTH), lambda i: (0, 0), pipeline_mode=pl.Buffered(1)),
        ],
        out_specs=[pl.BlockSpec((IN_TM, wd), lambda i: (i, 0)) for wd in _IN_OUT_WIDTHS],
        out_shape=[jax.ShapeDtypeStruct((t, wd), BF16) for wd in _IN_OUT_WIDTHS],
        scratch_shapes=[pltpu.VMEM((IN_TM, D_MODEL), BF16)],
        compiler_params=pltpu.CompilerParams(
            dimension_semantics=("arbitrary",), vmem_limit_bytes=VMEM_LIMIT),
        name="in_proj",
    )(h, gain, w)


ATT_PAD = HALF_WIDTH * max(DILATIONS)
PREP_ROWS = 512


def _attn_kernel(q_ref, k_ref, v_ref, cos_ref, sup_ref, sdn_ref, o_ref,
                 qs, kp, vp, acc, mrun, lrun, band):
    s_len = q_ref.shape[0]
    zeros = jnp.zeros((ATT_PAD, LANES), F32)
    kp[0:ATT_PAD, :] = zeros
    kp[ATT_PAD + s_len:, :] = zeros
    vp[0:ATT_PAD, :] = zeros
    vp[ATT_PAD + s_len:, :] = zeros

    d = (lax.broadcasted_iota(jnp.int32, (QBLK, KBLK), 1)
         - lax.broadcasted_iota(jnp.int32, (QBLK, KBLK), 0))
    band[...] = jnp.where((d >= 0) & (d <= 2 * HALF_WIDTH), 0.0, MASK_VALUE).astype(F32)

    def prep(j, carry):
        r0 = pl.multiple_of(j * PREP_ROWS, PREP_ROWS)
        rows = pl.ds(r0, PREP_ROWS)
        cos, sup, sdn = cos_ref[rows, :], sup_ref[rows, :], sdn_ref[rows, :]
        half = ATT_ROPE_DIM // 2
        q = _rotate(q_ref[rows, :].astype(F32), cos, sup, sdn, half)
        qs[rows, :] = q * (ATT_HEAD_DIM ** -0.5)
        kp[pl.ds(ATT_PAD + r0, PREP_ROWS), :] = _rotate(k_ref[rows, :].astype(F32), cos, sup, sdn, half)
        vp[pl.ds(ATT_PAD + r0, PREP_ROWS), :] = v_ref[rows, :].astype(F32)
        return carry

    lax.fori_loop(0, s_len // PREP_ROWS, prep, 0)

    lane_q = lax.broadcasted_iota(jnp.int32, (QBLK, LANES), 1)
    lane_k = lax.broadcasted_iota(jnp.int32, (KBLK, LANES), 1)
    head_q = (lane_q < ATT_HEAD_DIM, lane_q >= ATT_HEAD_DIM)
    head_k = (lane_k < ATT_HEAD_DIM, lane_k >= ATT_HEAD_DIM)
    key_j = lax.broadcasted_iota(jnp.int32, (1, KBLK), 1)

    def run_group(r, first):
        sub_len = s_len // r
        log2r = r.bit_length() - 1

        def body(t, carry):
            c = jnp.bitwise_and(t, r - 1)
            i = lax.shift_right_logical(t, log2r)
            q_start = c + (r * QBLK) * i
            k_start = ATT_PAD - r * HALF_WIDTH + q_start
            if r == 1:
                q_rows, k_rows = pl.ds(q_start, QBLK), pl.ds(k_start, KBLK)
            else:
                q_rows = pl.ds(q_start, QBLK, stride=r)
                k_rows = pl.ds(k_start, KBLK, stride=r)
            qb = qs[q_rows, :]
            kb = kp[k_rows, :].astype(BF16)
            vb = vp[k_rows, :]
            key_pos = QBLK * i - HALF_WIDTH + key_j
            bias = band[...] + jnp.where((key_pos >= 0) & (key_pos < sub_len), 0.0, MASK_VALUE)
            parts = []
            for hh in range(2):
                qh = jnp.where(head_q[hh], qb, 0.0).astype(BF16)
                s = lax.dot_general(qh, kb, (((1,), (1,)), ((), ())),
                                    preferred_element_type=F32) + bias
                m = jnp.max(s, axis=-1, keepdims=True)
                p = jnp.exp(s - m).astype(BF16)
                vh = jnp.where(head_k[hh], vb, 1.0).astype(BF16)
                parts.append((jnp.dot(p, vh, preferred_element_type=F32), m))
            (n0, m0), (n1, m1) = parts
            num = jnp.where(head_q[0], n0, n1)
            den = pltpu.roll(jnp.where(head_q[0], n1, n0), ATT_HEAD_DIM, 1)
            mx = jnp.where(head_q[0], m0, m1)
            if first:
                acc[q_rows, :] = num
                lrun[q_rows, :] = den
                mrun[q_rows, :] = mx
            else:
                m_old = mrun[q_rows, :]
                m_new = jnp.maximum(m_old, mx)
                a_old = jnp.exp(m_old - m_new)
                a_new = jnp.exp(mx - m_new)
                acc[q_rows, :] = a_old * acc[q_rows, :] + a_new * num
                lrun[q_rows, :] = a_old * lrun[q_rows, :] + a_new * den
                mrun[q_rows, :] = m_new
            return carry

        lax.fori_loop(0, s_len // QBLK, body, 0)

    for g, r in enumerate(DILATIONS):
        run_group(r, g == 0)

    def finish(j, carry):
        rows = pl.ds(pl.multiple_of(j * PREP_ROWS, PREP_ROWS), PREP_ROWS)
        o_ref[rows, :] = (acc[rows, :] / lrun[rows, :]).astype(o_ref.dtype)
        return carry

    lax.fori_loop(0, s_len // PREP_ROWS, finish, 0)


def _attention(qkv, cos, sup, sdn):
    b, s, _ = qkv.shape
    pairs = ATT_W // LANES

    def col(section):
        return pl.BlockSpec((None, s, LANES), lambda bi, hp: (bi, 0, section * pairs + hp))

    table = pl.BlockSpec((s, LANES), lambda bi, hp: (0, 0), pipeline_mode=pl.Buffered(1))
    return pl.pallas_call(
        _attn_kernel,
        grid=(b, pairs),
        in_specs=[col(0), col(1), col(2), table, table, table],
        out_specs=pl.BlockSpec((None, s, LANES), lambda bi, hp: (bi, 0, hp)),
        out_shape=jax.ShapeDtypeStruct((b, s, ATT_W), BF16),
        scratch_shapes=[
            pltpu.VMEM((s, LANES), F32),
            pltpu.VMEM((s + 2 * ATT_PAD, LANES), F32),
            pltpu.VMEM((s + 2 * ATT_PAD, LANES), F32),
            pltpu.VMEM((s, LANES), F32),
            pltpu.VMEM((s, LANES), F32),
            pltpu.VMEM((s, LANES), F32),
            pltpu.VMEM((QBLK, KBLK), F32),
        ],
        compiler_params=pltpu.CompilerParams(
            dimension_semantics=("arbitrary", "arbitrary"), vmem_limit_bytes=VMEM_LIMIT),
        name="attention",
    )(qkv, qkv, qkv, cos, sup, sdn)


def _ret_kernel(q_ref, k_ref, v_ref, g_ref, dl_ref, cos_ref, sup_ref, sdn_ref, o_ref,
                qs, ks, ys):
    s_len = q_ref.shape[0]
    n_chunks = s_len // RET_CHUNK
    c = RET_CHUNK

    def prep(j, carry):
        rows = pl.ds(pl.multiple_of(j * PREP_ROWS, PREP_ROWS), PREP_ROWS)
        cos, sup, sdn = cos_ref[rows, :], sup_ref[rows, :], sdn_ref[rows, :]
        half = RET_QK_DIM // 2
        qs[rows, :] = _rotate(q_ref[rows, :].astype(F32), cos, sup, sdn, half)
        ks[rows, :] = _rotate(k_ref[rows, :].astype(F32), cos, sup, sdn, half) * (RET_QK_DIM ** -0.5)
        return carry

    lax.fori_loop(0, s_len // PREP_ROWS, prep, 0)

    x = dl_ref[...]
    log_gamma = jnp.minimum(x, 0.0) - jnp.log(1.0 + jnp.exp(-jnp.abs(x)))
    row = lax.broadcasted_iota(jnp.int32, (c, LANES), 0).astype(F32)
    lane = lax.broadcasted_iota(jnp.int32, (c, LANES), 1)
    diff = row - lane.astype(F32)
    head_mask = (lane < RET_QK_DIM, lane >= RET_QK_DIM)

    decay, xi_f, zeta_f, xi_b, zeta_b, gc_f, gc_b = [], [], [], [], [], [], []
    for hh in range(2):
        lgf = log_gamma[0:1, hh * LANES:(hh + 1) * LANES]
        lgb = log_gamma[1:2, hh * LANES:(hh + 1) * LANES]
        decay.append(jnp.exp(jnp.where(diff >= 0, diff * lgf, -diff * lgb)))
        xi_f.append(jnp.exp((row + 1.0) * lgf))
        zeta_f.append(jnp.exp((c - 1.0 - row) * lgf))
        xi_b.append(jnp.exp((c - row) * lgb))
        zeta_b.append(jnp.exp(row * lgb))
        gc_f.append(jnp.exp(c * lgf))
        gc_b.append(jnp.exp(c * lgb))

    nt = (((1,), (1,)), ((), ()))
    tn = (((0,), (0,)), ((), ()))

    def forward(t, states):
        rows = pl.ds(pl.multiple_of(t * c, c), c)
        q = qs[rows, :]
        k = ks[rows, :]
        kb = k.astype(BF16)
        new_states = []
        for hh in range(2):
            cols = slice(hh * LANES, (hh + 1) * LANES)
            qh = jnp.where(head_mask[hh], q, 0.0)
            vh = v_ref[rows, cols]
            s = lax.dot_general(qh.astype(BF16), kb, nt, preferred_element_type=F32) * decay[hh]
            y = jnp.dot(s.astype(BF16), vh, preferred_element_type=F32)
            y = y + jnp.dot((qh * xi_f[hh]).astype(BF16), states[hh].astype(BF16),
                            preferred_element_type=F32)
            ys[rows, cols] = y
            kv = lax.dot_general((k * zeta_f[hh]).astype(BF16), vh, tn, preferred_element_type=F32)
            new_states.append(gc_f[hh] * states[hh] + kv)
        return tuple(new_states)

    zero_state = jnp.zeros((LANES, LANES), F32)
    lax.fori_loop(0, n_chunks, forward, (zero_state, zero_state))

    def backward(tt, states):
        t = n_chunks - 1 - tt
        rows = pl.ds(pl.multiple_of(t * c, c), c)
        q = qs[rows, :]
        k = ks[rows, :]
        new_states = []
        for hh in range(2):
            cols = slice(hh * LANES, (hh + 1) * LANES)
            qh = jnp.where(head_mask[hh], q, 0.0)
            vh = v_ref[rows, cols]
            y = ys[rows, cols] + jnp.dot((qh * xi_b[hh]).astype(BF16), states[hh].astype(BF16),
                                         preferred_element_type=F32)
            mu = jnp.mean(y, axis=-1, keepdims=True)
            yc = y - mu
            var = jnp.mean(yc * yc, axis=-1, keepdims=True)
            yn = yc * lax.rsqrt(var + GN_EPS)
            g = g_ref[rows, cols].astype(F32)
            o_ref[rows, cols] = (g * _sigmoid(g) * yn).astype(o_ref.dtype)
            kv = lax.dot_general((k * zeta_b[hh]).astype(BF16), vh, tn, preferred_element_type=F32)
            new_states.append(gc_b[hh] * states[hh] + kv)
        return tuple(new_states)

    lax.fori_loop(0, n_chunks, backward, (zero_state, zero_state))


def _retention(qk, v, g, decay_lanes, cos, sup, sdn):
    b, s, _ = qk.shape
    pairs = RET_QK_W // LANES
    table = pl.BlockSpec((s, LANES), lambda bi, hp: (0, 0), pipeline_mode=pl.Buffered(1))
    wide = pl.BlockSpec((None, s, 2 * LANES), lambda bi, hp: (bi, 0, hp))
    return pl.pallas_call(
        _ret_kernel,
        grid=(b, pairs),
        in_specs=[
            pl.BlockSpec((None, s, LANES), lambda bi, hp: (bi, 0, hp)),
            pl.BlockSpec((None, s, LANES), lambda bi, hp: (bi, 0, pairs + hp)),
            wide, wide,
            pl.BlockSpec((2, 2 * LANES), lambda bi, hp: (0, hp)),
            table, table, table,
        ],
        out_specs=wide,
        out_shape=jax.ShapeDtypeStruct((b, s, RET_V_W), BF16),
        scratch_shapes=[
            pltpu.VMEM((s, LANES), F32),
            pltpu.VMEM((s, LANES), F32),
            pltpu.VMEM((s, 2 * LANES), F32),
        ],
        compiler_params=pltpu.CompilerParams(
            dimension_semantics=("arbitrary", "arbitrary"), vmem_limit_bytes=VMEM_LIMIT),
        name="retention",
    )(qk, qk, v, g, decay_lanes, cos, sup, sdn)


POST_TM = 256
FF_CHUNK = 1024


def _post_kernel(h_ref, att_ref, ret_ref, ga_ref, gb_ref, p_ref,
                 w_att_ref, w_ret_ref, w_out_ref, w_up_ref, w_down_ref, w_gate_ref, w_proj_ref,
                 n_post_ref, n_mlp_pre_ref, n_mlp_post_ref, n_ple_ref, o_ref, u_ref):
    a = jnp.dot(att_ref[...], w_att_ref[...], preferred_element_type=F32)
    r = jnp.dot(ret_ref[...], w_ret_ref[...], preferred_element_type=F32)
    merged = (_sigmoid(ga_ref[...].astype(F32)) * a + _sigmoid(gb_ref[...].astype(F32)) * r)
    y = jnp.dot(merged.astype(BF16), w_out_ref[...], preferred_element_type=F32)
    h = h_ref[...] + _rms(y) * n_post_ref[...]

    u_ref[...] = (_rms(h) * n_mlp_pre_ref[...]).astype(BF16)
    ff = jnp.zeros((POST_TM, D_MODEL), F32)
    for c in range(0, D_FF, FF_CHUNK):
        z = jnp.dot(u_ref[...], w_up_ref[:, c:c + FF_CHUNK], preferred_element_type=F32)
        z = jnp.maximum(z, 0.0)
        ff = ff + jnp.dot((z * z).astype(BF16), w_down_ref[c:c + FF_CHUNK, :],
                          preferred_element_type=F32)
    h = h + _rms(ff) * n_mlp_post_ref[...]

    gate = _sigmoid(jnp.dot(_rms(h).astype(BF16), w_gate_ref[...], preferred_element_type=F32))
    pe = jnp.dot(p_ref[...].astype(BF16), w_proj_ref[...], preferred_element_type=F32)
    o_ref[...] = h + gate * (_rms(pe) * n_ple_ref[...])


def _post(h, att, ret, ga, gb, p, w_att, w_ret, w_out, w_up, w_down, w_gate, w_proj,
          n_post, n_mlp_pre, n_mlp_post, n_ple):
    t = h.shape[0]

    def rows(width):
        return pl.BlockSpec((POST_TM, width), lambda i: (i, 0))

    def whole(arr):
        return pl.BlockSpec(arr.shape, lambda i: (0, 0), pipeline_mode=pl.Buffered(1))

    weights = (w_att, w_ret, w_out, w_up, w_down, w_gate, w_proj)
    gains = (n_post, n_mlp_pre, n_mlp_post, n_ple)
    return pl.pallas_call(
        _post_kernel,
        grid=(t // POST_TM,),
        in_specs=[rows(D_MODEL), rows(ATT_W), rows(RET_V_W), rows(D_MODEL), rows(D_MODEL),
                  rows(PLE_DIM)] + [whole(w) for w in weights] + [whole(g) for g in gains],
        out_specs=rows(D_MODEL),
        out_shape=jax.ShapeDtypeStruct((t, D_MODEL), F32),
        scratch_shapes=[pltpu.VMEM((POST_TM, D_MODEL), BF16)],
        compiler_params=pltpu.CompilerParams(
            dimension_semantics=("arbitrary",), vmem_limit_bytes=VMEM_LIMIT),
        name="post",
    )(h, att, ret, ga, gb, p, *weights, *gains)


def _rotary_tables(s, rot_dim, head_dim, theta):
    half = rot_dim // 2
    inv = theta ** (-jnp.arange(half, dtype=F32) * 2.0 / rot_dim)
    ang = jnp.arange(s).astype(F32)[:, None] * inv[None, :]
    within = jnp.arange(LANES) % head_dim
    cos = jnp.cos(ang)[:, within % half]
    sin = jnp.sin(ang)[:, within % half]
    rotated = within < rot_dim
    lower = within < half
    cos_t = jnp.where(rotated, cos, 1.0)
    sin_up = jnp.where(lower, -sin, 0.0)
    sin_dn = jnp.where(rotated & ~lower, sin, 0.0)
    return cos_t, sin_up, sin_dn


def kernel(x, p, w_in, w_att_out, w_ret_out, w_out, w_mlp_up, w_mlp_down, w_ple_gate, w_ple_proj,
           ret_decay_logit, norm_mix_pre, norm_mix_post, norm_mlp_pre, norm_mlp_post, norm_ple):
    b, s, d = x.shape
    depth = w_in.shape[0]
    t = b * s
    att_tables = _rotary_tables(s, ATT_ROPE_DIM, ATT_HEAD_DIM, ATT_ROPE_THETA)
    ret_tables = _rotary_tables(s, RET_QK_DIM, RET_QK_DIM, RET_THETA)

    h = x.reshape(t, d)
    for i in range(depth):
        def gain(a):
            return a[i].reshape(1, d)

        qkv, rqk, rv, rg, ga, gb = _in_proj(h, gain(norm_mix_pre), w_in[i].astype(BF16))
        att = _attention(qkv.reshape(b, s, -1), *att_tables)
        decay_lanes = jnp.repeat(ret_decay_logit[i], RET_V_DIM, axis=-1)
        ret = _retention(rqk.reshape(b, s, -1), rv.reshape(b, s, -1), rg.reshape(b, s, -1),
                         decay_lanes, *ret_tables)
        h = _post(h, att.reshape(t, -1), ret.reshape(t, -1), ga, gb, p[i].reshape(t, -1),
                  w_att_out[i].astype(BF16), w_ret_out[i].astype(BF16), w_out[i].astype(BF16),
                  w_mlp_up[i].astype(BF16), w_mlp_down[i].astype(BF16),
                  w_ple_gate[i].astype(BF16), w_ple_proj[i].astype(BF16),
                  gain(norm_mix_post), gain(norm_mlp_pre), gain(norm_mlp_post), gain(norm_ple))
    return h.reshape(b, s, d)
```

```python
import functools

import jax
import jax.numpy as jnp
from jax import lax
from jax.experimental import pallas as pl
from jax.experimental.pallas import tpu as pltpu

D_MODEL = 1024
ATT_HEADS = 8
ATT_HEAD_DIM = 64
ATT_ROPE_DIM = 16
ATT_ROPE_THETA = 500000.0
DILATIONS = (1, 4, 16)
HALF_WIDTH = 64
RET_HEADS = 8
RET_QK_DIM = 64
RET_V_DIM = 128
RET_THETA = 10000.0
RET_CHUNK = 128
D_FF = 4096
PLE_DIM = 256
NORM_EPS = 1e-6
GN_EPS = 1e-5
MASK_VALUE = -1e30

ATT_W = ATT_HEADS * ATT_HEAD_DIM
RET_QK_W = RET_HEADS * RET_QK_DIM
RET_V_W = RET_HEADS * RET_V_DIM
IN_WIDTH = 3 * ATT_W + 2 * RET_QK_W + 2 * RET_V_W + 2 * D_MODEL

LANES = 128
QBLK = 128
KBLK = QBLK + 2 * HALF_WIDTH
VMEM_LIMIT = 56 * 1024 * 1024

F32 = jnp.float32
BF16 = jnp.bfloat16


def _rms(x):
    return x * lax.rsqrt(jnp.mean(x * x, axis=-1, keepdims=True) + NORM_EPS)


def _sigmoid(x):
    return 1.0 / (1.0 + jnp.exp(-x))


def _rotate(x, cos, sin_up, sin_dn, shift):
    return (x * cos + pltpu.roll(x, LANES - shift, 1) * sin_up
            + pltpu.roll(x, shift, 1) * sin_dn)


IN_TM = 512
IN_TN = 512
_IN_OUT_WIDTHS = (3 * ATT_W, 2 * RET_QK_W, RET_V_W, RET_V_W, D_MODEL, D_MODEL)


def _in_proj_kernel(x_ref, g_ref, w_ref, *rest):
    out_refs, u_ref = rest[:-1], rest[-1]
    u_ref[...] = (_rms(x_ref[...]) * g_ref[...]).astype(BF16)
    off = 0
    for o_ref, width in zip(out_refs, _IN_OUT_WIDTHS):
        for c in range(0, width, IN_TN):
            o_ref[:, c:c + IN_TN] = jnp.dot(
                u_ref[...], w_ref[:, off + c:off + c + IN_TN],
                preferred_element_type=F32).astype(BF16)
        off += width


def _in_proj(h, gain, w):
    t = h.shape[0]
    return pl.pallas_call(
        _in_proj_kernel,
        grid=(t // IN_TM,),
        in_specs=[
            pl.BlockSpec((IN_TM, D_MODEL), lambda i: (i, 0)),
            pl.BlockSpec((1, D_MODEL), lambda i: (0, 0)),
            pl.BlockSpec((D_MODEL, IN_WIDTH), lambda i: (0, 0), pipeline_mode=pl.Buffered(1)),
        ],
        out_specs=[pl.BlockSpec((IN_TM, wd), lambda i: (i, 0)) for wd in _IN_OUT_WIDTHS],
        out_shape=[jax.ShapeDtypeStruct((t, wd), BF16) for wd in _IN_OUT_WIDTHS],
        scratch_shapes=[pltpu.VMEM((IN_TM, D_MODEL), BF16)],
        compiler_params=pltpu.CompilerParams(
            dimension_semantics=("arbitrary",), vmem_limit_bytes=VMEM_LIMIT),
        name="in_proj",
    )(h, gain, w)


ATT_PAD = HALF_WIDTH * max(DILATIONS)
PREP_ROWS = 512
ATT_Q_SCALE = ATT_HEAD_DIM ** -0.5 * 1.4426950408889634
ATT_UNROLL = 4


def _attn_kernel(q_ref, k_ref, v_ref, cos_ref, sup_ref, sdn_ref, o_ref,
                 qs, kp, vp, acc, mrun, lrun, band):
    s_len = q_ref.shape[0]
    zeros = jnp.zeros((ATT_PAD, LANES), F32)
    kp[0:ATT_PAD, :] = zeros
    kp[ATT_PAD + s_len:, :] = zeros
    vp[0:ATT_PAD, :] = zeros
    vp[ATT_PAD + s_len:, :] = zeros

    d = (lax.broadcasted_iota(jnp.int32, (QBLK, KBLK), 1)
         - lax.broadcasted_iota(jnp.int32, (QBLK, KBLK), 0))
    band[...] = jnp.where((d >= 0) & (d <= 2 * HALF_WIDTH), 0.0, MASK_VALUE).astype(F32)

    def prep(j, carry):
        r0 = pl.multiple_of(j * PREP_ROWS, PREP_ROWS)
        rows = pl.ds(r0, PREP_ROWS)
        cos, sup, sdn = cos_ref[rows, :], sup_ref[rows, :], sdn_ref[rows, :]
        half = ATT_ROPE_DIM // 2
        q = _rotate(q_ref[rows, :].astype(F32), cos, sup, sdn, half)
        qs[rows, :] = q * ATT_Q_SCALE
        kp[pl.ds(ATT_PAD + r0, PREP_ROWS), :] = _rotate(k_ref[rows, :].astype(F32), cos, sup, sdn, half)
        vp[pl.ds(ATT_PAD + r0, PREP_ROWS), :] = v_ref[rows, :].astype(F32)
        return carry

    lax.fori_loop(0, s_len // PREP_ROWS, prep, 0)

    lane_q = lax.broadcasted_iota(jnp.int32, (QBLK, LANES), 1)
    head_q = (lane_q < ATT_HEAD_DIM, lane_q >= ATT_HEAD_DIM)
    key_j = lax.broadcasted_iota(jnp.int32, (1, KBLK), 1)

    def run_group(r, first):
        sub_len = s_len // r
        log2r = r.bit_length() - 1

        def body(t, carry):
            c = jnp.bitwise_and(t, r - 1)
            i = lax.shift_right_logical(t, log2r)
            q_start = c + (r * QBLK) * i
            k_start = ATT_PAD - r * HALF_WIDTH + q_start
            if r == 1:
                q_rows, k_rows = pl.ds(q_start, QBLK), pl.ds(k_start, KBLK)
            else:
                q_rows = pl.ds(q_start, QBLK, stride=r)
                k_rows = pl.ds(k_start, KBLK, stride=r)
            qb = qs[q_rows, :]
            kb = kp[k_rows, :].astype(BF16)
            vb = vp[k_rows, :].astype(BF16)
            key_pos = QBLK * i - HALF_WIDTH + key_j
            bias = band[...] + jnp.where((key_pos >= 0) & (key_pos < sub_len), 0.0, MASK_VALUE)
            q2 = jnp.concatenate([jnp.where(head_q[0], qb, 0.0), jnp.where(head_q[1], qb, 0.0)],
                                 axis=0).astype(BF16)
            s = lax.dot_general(q2, kb, (((1,), (1,)), ((), ())), preferred_element_type=F32)
            s = s + jnp.concatenate([bias, bias], axis=0)
            m = jnp.max(s, axis=-1, keepdims=True)
            p = jnp.exp2(s - m).astype(BF16)
            v2 = jnp.concatenate([vb, jnp.ones_like(vb)], axis=1)
            nd = jnp.dot(p, v2, preferred_element_type=F32)
            num = jnp.where(head_q[0], nd[:QBLK, :LANES], nd[QBLK:, :LANES])
            den = jnp.where(head_q[0], nd[:QBLK, LANES:], nd[QBLK:, LANES:])
            mx = jnp.where(head_q[0], m[:QBLK], m[QBLK:])
            if first:
                acc[q_rows, :] = num
                lrun[q_rows, :] = den
                mrun[q_rows, :] = mx
            else:
                m_old = mrun[q_rows, :]
                m_new = jnp.maximum(m_old, mx)
                a_old = jnp.exp2(m_old - m_new)
                a_new = jnp.exp2(mx - m_new)
                acc[q_rows, :] = a_old * acc[q_rows, :] + a_new * num
                lrun[q_rows, :] = a_old * lrun[q_rows, :] + a_new * den
                mrun[q_rows, :] = m_new
            return carry

        lax.fori_loop(0, s_len // QBLK, body, 0, unroll=ATT_UNROLL)

    for g, r in enumerate(DILATIONS):
        run_group(r, g == 0)

    def finish(j, carry):
        rows = pl.ds(pl.multiple_of(j * PREP_ROWS, PREP_ROWS), PREP_ROWS)
        o_ref[rows, :] = (acc[rows, :] / lrun[rows, :]).astype(o_ref.dtype)
        return carry

    lax.fori_loop(0, s_len // PREP_ROWS, finish, 0)


def _attention(qkv, cos, sup, sdn):
    b, s, _ = qkv.shape
    pairs = ATT_W // LANES

    def col(section):
        return pl.BlockSpec((None, s, LANES), lambda bi, hp: (bi, 0, section * pairs + hp))

    table = pl.BlockSpec((s, LANES), lambda bi, hp: (0, 0), pipeline_mode=pl.Buffered(1))
    return pl.pallas_call(
        _attn_kernel,
        grid=(b, pairs),
        in_specs=[col(0), col(1), col(2), table, table, table],
        out_specs=pl.BlockSpec((None, s, LANES), lambda bi, hp: (bi, 0, hp)),
        out_shape=jax.ShapeDtypeStruct((b, s, ATT_W), BF16),
        scratch_shapes=[
            pltpu.VMEM((s, LANES), F32),
            pltpu.VMEM((s + 2 * ATT_PAD, LANES), F32),
            pltpu.VMEM((s + 2 * ATT_PAD, LANES), F32),
            pltpu.VMEM((s, LANES), F32),
            pltpu.VMEM((s, LANES), F32),
            pltpu.VMEM((s, LANES), F32),
            pltpu.VMEM((QBLK, KBLK), F32),
        ],
        compiler_params=pltpu.CompilerParams(
            dimension_semantics=("arbitrary", "arbitrary"), vmem_limit_bytes=VMEM_LIMIT),
        name="attention",
    )(qkv, qkv, qkv, cos, sup, sdn)


RET_UNROLL = 2


def _ret_kernel(q_ref, k_ref, v_ref, g_ref, dl_ref, cos_ref, sup_ref, sdn_ref, o_ref,
                qs, kt, states, sf, sb):
    s_len = q_ref.shape[0]
    c = RET_CHUNK
    n_chunks = s_len // c
    half = RET_QK_DIM // 2

    def prep(t, carry):
        rows = pl.ds(pl.multiple_of(t * c, c), c)
        cos, sup, sdn = cos_ref[rows, :], sup_ref[rows, :], sdn_ref[rows, :]
        qs[rows, :] = _rotate(q_ref[rows, :].astype(F32), cos, sup, sdn, half)
        k = _rotate(k_ref[rows, :].astype(F32), cos, sup, sdn, half) * (RET_QK_DIM ** -0.5)
        kt[t] = k.T
        return carry

    lax.fori_loop(0, n_chunks, prep, 0, unroll=2)

    x = dl_ref[...]
    log_gamma = jnp.minimum(x, 0.0) - jnp.log(1.0 + jnp.exp(-jnp.abs(x)))
    lgf, lgb = log_gamma[0:1, :], log_gamma[1:2, :]
    row_i = lax.broadcasted_iota(jnp.int32, (c, LANES), 0)
    lane_i = lax.broadcasted_iota(jnp.int32, (c, LANES), 1)
    row, lane = row_i.astype(F32), lane_i.astype(F32)
    head_lane = (lane_i < RET_QK_DIM, lane_i >= RET_QK_DIM)

    def per_row(lg):
        return jnp.where(row_i < RET_QK_DIM, lg[:, :LANES], lg[:, LANES:])

    def per_lane(lg):
        return jnp.where(head_lane[0], lg[:, :LANES], lg[:, LANES:])

    zeta_f = jnp.exp((c - 1.0 - lane) * per_row(lgf))
    zeta_b = jnp.exp(lane * per_row(lgb))
    xi_f = jnp.exp((row + 1.0) * per_lane(lgf))
    xi_b = jnp.exp((c - row) * per_lane(lgb))
    diff = row - lane
    decay = jnp.concatenate(
        [jnp.exp(jnp.where(diff >= 0, diff * lgf[:, h * LANES:(h + 1) * LANES],
                           -diff * lgb[:, h * LANES:(h + 1) * LANES])) for h in range(2)], axis=0)

    srow = lax.broadcasted_iota(jnp.int32, (LANES, 2 * LANES), 0)
    scol = lax.broadcasted_iota(jnp.int32, (LANES, 2 * LANES), 1)
    same_head = (srow < RET_QK_DIM) == (scol < LANES)
    gc_f = jnp.where(same_head, jnp.exp(c * lgf), 0.0)
    gc_b = jnp.where(same_head, jnp.exp(c * lgb), 0.0)

    sf[...] = jnp.zeros_like(sf)
    sb[...] = jnp.zeros_like(sb)

    def scan(t, carry):
        for state, zeta, gc, tt, half_rows in ((sf, zeta_f, gc_f, t, slice(0, LANES)),
                                               (sb, zeta_b, gc_b, n_chunks - 1 - t, slice(LANES, 2 * LANES))):
            rows = pl.ds(pl.multiple_of(tt * c, c), c)
            old = state[...]
            states[tt, half_rows, :] = old.astype(BF16)
            kv = jnp.dot((kt[tt] * zeta).astype(BF16), v_ref[rows, :], preferred_element_type=F32)
            state[...] = gc * old + jnp.where(same_head, kv, 0.0)
        return carry

    lax.fori_loop(0, n_chunks, scan, 0, unroll=RET_UNROLL)

    vrow = lax.broadcasted_iota(jnp.int32, (2 * c, 2 * LANES), 0)
    vcol = lax.broadcasted_iota(jnp.int32, (2 * c, 2 * LANES), 1)
    v_diag = (vrow < c) == (vcol < LANES)

    def combine(t, carry):
        rows = pl.ds(pl.multiple_of(t * c, c), c)
        q = qs[rows, :]
        q2 = jnp.concatenate([jnp.where(head_lane[0], q, 0.0), jnp.where(head_lane[1], q, 0.0)],
                             axis=0).astype(BF16)
        s = jnp.dot(q2, kt[t].astype(BF16), preferred_element_type=F32) * decay
        s_wide = jnp.concatenate([s[:c], s[c:]], axis=1).astype(BF16)
        v = v_ref[rows, :]
        v_blocks = jnp.where(v_diag, jnp.concatenate([v, v], axis=0), 0.0)
        y = jnp.dot(s_wide, v_blocks, preferred_element_type=F32)
        q_cross = jnp.concatenate([q * xi_f, q * xi_b], axis=1).astype(BF16)
        y = y + jnp.dot(q_cross, states[t], preferred_element_type=F32)
        for hh in range(2):
            cols = slice(hh * LANES, (hh + 1) * LANES)
            yh = y[:, cols]
            mu = jnp.mean(yh, axis=-1, keepdims=True)
            yc = yh - mu
            var = jnp.mean(yc * yc, axis=-1, keepdims=True)
            yn = yc * lax.rsqrt(var + GN_EPS)
            g = g_ref[rows, cols].astype(F32)
            o_ref[rows, cols] = (g * _sigmoid(g) * yn).astype(o_ref.dtype)
        return carry

    lax.fori_loop(0, n_chunks, combine, 0, unroll=RET_UNROLL)


def _retention(qk, v, g, decay_lanes, cos, sup, sdn):
    b, s, _ = qk.shape
    pairs = RET_QK_W // LANES
    n_chunks = s // RET_CHUNK
    table = pl.BlockSpec((s, LANES), lambda bi, hp: (0, 0), pipeline_mode=pl.Buffered(1))
    wide = pl.BlockSpec((None, s, 2 * LANES), lambda bi, hp: (bi, 0, hp))
    return pl.pallas_call(
        _ret_kernel,
        grid=(b, pairs),
        in_specs=[
            pl.BlockSpec((None, s, LANES), lambda bi, hp: (bi, 0, hp)),
            pl.BlockSpec((None, s, LANES), lambda bi, hp: (bi, 0, pairs + hp)),
            wide, wide,
            pl.BlockSpec((2, 2 * LANES), lambda bi, hp: (0, hp)),
            table, table, table,
        ],
        out_specs=wide,
        out_shape=jax.ShapeDtypeStruct((b, s, RET_V_W), BF16),
        scratch_shapes=[
            pltpu.VMEM((s, LANES), F32),
            pltpu.VMEM((n_chunks, LANES, RET_CHUNK), F32),
            pltpu.VMEM((n_chunks, 2 * LANES, 2 * LANES), BF16),
            pltpu.VMEM((LANES, 2 * LANES), F32),
            pltpu.VMEM((LANES, 2 * LANES), F32),
        ],
        compiler_params=pltpu.CompilerParams(
            dimension_semantics=("arbitrary", "arbitrary"), vmem_limit_bytes=VMEM_LIMIT),
        name="retention",
    )(qk, qk, v, g, decay_lanes, cos, sup, sdn)


POST_TM = 256
FF_CHUNK = 1024


def _post_kernel(h_ref, att_ref, ret_ref, ga_ref, gb_ref, p_ref,
                 w_att_ref, w_ret_ref, w_out_ref, w_up_ref, w_down_ref, w_gate_ref, w_proj_ref,
                 n_post_ref, n_mlp_pre_ref, n_mlp_post_ref, n_ple_ref, o_ref, u_ref):
    a = jnp.dot(att_ref[...], w_att_ref[...], preferred_element_type=F32)
    r = jnp.dot(ret_ref[...], w_ret_ref[...], preferred_element_type=F32)
    merged = (_sigmoid(ga_ref[...].astype(F32)) * a + _sigmoid(gb_ref[...].astype(F32)) * r)
    y = jnp.dot(merged.astype(BF16), w_out_ref[...], preferred_element_type=F32)
    h = h_ref[...] + _rms(y) * n_post_ref[...]

    u_ref[...] = (_rms(h) * n_mlp_pre_ref[...]).astype(BF16)
    ff = jnp.zeros((POST_TM, D_MODEL), F32)
    for c in range(0, D_FF, FF_CHUNK):
        z = jnp.dot(u_ref[...], w_up_ref[:, c:c + FF_CHUNK], preferred_element_type=F32)
        z = jnp.maximum(z, 0.0)
        ff = ff + jnp.dot((z * z).astype(BF16), w_down_ref[c:c + FF_CHUNK, :],
                          preferred_element_type=F32)
    h = h + _rms(ff) * n_mlp_post_ref[...]

    gate = _sigmoid(jnp.dot(_rms(h).astype(BF16), w_gate_ref[...], preferred_element_type=F32))
    pe = jnp.dot(p_ref[...].astype(BF16), w_proj_ref[...], preferred_element_type=F32)
    o_ref[...] = h + gate * (_rms(pe) * n_ple_ref[...])


def _post(h, att, ret, ga, gb, p, w_att, w_ret, w_out, w_up, w_down, w_gate, w_proj,
          n_post, n_mlp_pre, n_mlp_post, n_ple):
    t = h.shape[0]

    def rows(width):
        return pl.BlockSpec((POST_TM, width), lambda i: (i, 0))

    def whole(arr):
        return pl.BlockSpec(arr.shape, lambda i: (0, 0), pipeline_mode=pl.Buffered(1))

    weights = (w_att, w_ret, w_out, w_up, w_down, w_gate, w_proj)
    gains = (n_post, n_mlp_pre, n_mlp_post, n_ple)
    return pl.pallas_call(
        _post_kernel,
        grid=(t // POST_TM,),
        in_specs=[rows(D_MODEL), rows(ATT_W), rows(RET_V_W), rows(D_MODEL), rows(D_MODEL),
                  rows(PLE_DIM)] + [whole(w) for w in weights] + [whole(g) for g in gains],
        out_specs=rows(D_MODEL),
        out_shape=jax.ShapeDtypeStruct((t, D_MODEL), F32),
        scratch_shapes=[pltpu.VMEM((POST_TM, D_MODEL), BF16)],
        compiler_params=pltpu.CompilerParams(
            dimension_semantics=("arbitrary",), vmem_limit_bytes=VMEM_LIMIT),
        name="post",
    )(h, att, ret, ga, gb, p, *weights, *gains)


def _rotary_tables(s, rot_dim, head_dim, theta):
    half = rot_dim // 2
    inv = theta ** (-jnp.arange(half, dtype=F32) * 2.0 / rot_dim)
    ang = jnp.arange(s).astype(F32)[:, None] * inv[None, :]
    within = jnp.arange(LANES) % head_dim
    cos = jnp.cos(ang)[:, within % half]
    sin = jnp.sin(ang)[:, within % half]
    rotated = within < rot_dim
    lower = within < half
    cos_t = jnp.where(rotated, cos, 1.0)
    sin_up = jnp.where(lower, -sin, 0.0)
    sin_dn = jnp.where(rotated & ~lower, sin, 0.0)
    return cos_t, sin_up, sin_dn


def kernel(x, p, w_in, w_att_out, w_ret_out, w_out, w_mlp_up, w_mlp_down, w_ple_gate, w_ple_proj,
           ret_decay_logit, norm_mix_pre, norm_mix_post, norm_mlp_pre, norm_mlp_post, norm_ple):
    b, s, d = x.shape
    depth = w_in.shape[0]
    t = b * s
    att_tables = _rotary_tables(s, ATT_ROPE_DIM, ATT_HEAD_DIM, ATT_ROPE_THETA)
    ret_tables = _rotary_tables(s, RET_QK_DIM, RET_QK_DIM, RET_THETA)

    h = x.reshape(t, d)
    for i in range(depth):
        def gain(a):
            return a[i].reshape(1, d)

        qkv, rqk, rv, rg, ga, gb = _in_proj(h, gain(norm_mix_pre), w_in[i].astype(BF16))
        att = _attention(qkv.reshape(b, s, -1), *att_tables)
        decay_lanes = jnp.repeat(ret_decay_logit[i], RET_V_DIM, axis=-1)
        ret = _retention(rqk.reshape(b, s, -1), rv.reshape(b, s, -1), rg.reshape(b, s, -1),
                         decay_lanes, *ret_tables)
        h = _post(h, att.reshape(t, -1), ret.reshape(t, -1), ga, gb, p[i].reshape(t, -1),
                  w_att_out[i].astype(BF16), w_ret_out[i].astype(BF16), w_out[i].astype(BF16),
                  w_mlp_up[i].astype(BF16), w_mlp_down[i].astype(BF16),
                  w_ple_gate[i].astype(BF16), w_ple_proj[i].astype(BF16),
                  gain(norm_mix_post), gain(norm_mlp_pre), gain(norm_mlp_post), gain(norm_ple))
    return h.reshape(b, s, d)
```

```python
import functools

import jax
import jax.numpy as jnp
from jax import lax
from jax.experimental import pallas as pl
from jax.experimental.pallas import tpu as pltpu

D_MODEL = 1024
ATT_HEADS = 8
ATT_HEAD_DIM = 64
ATT_ROPE_DIM = 16
ATT_ROPE_THETA = 500000.0
DILATIONS = (1, 4, 16)
HALF_WIDTH = 64
RET_HEADS = 8
RET_QK_DIM = 64
RET_V_DIM = 128
RET_THETA = 10000.0
RET_CHUNK = 128
D_FF = 4096
PLE_DIM = 256
NORM_EPS = 1e-6
GN_EPS = 1e-5
MASK_VALUE = -1e30

ATT_W = ATT_HEADS * ATT_HEAD_DIM
RET_QK_W = RET_HEADS * RET_QK_DIM
RET_V_W = RET_HEADS * RET_V_DIM
IN_WIDTH = 3 * ATT_W + 2 * RET_QK_W + 2 * RET_V_W + 2 * D_MODEL

LANES = 128
QBLK = 128
KBLK = QBLK + 2 * HALF_WIDTH
VMEM_LIMIT = 56 * 1024 * 1024

F32 = jnp.float32
BF16 = jnp.bfloat16


def _rms(x):
    return x * lax.rsqrt(jnp.mean(x * x, axis=-1, keepdims=True) + NORM_EPS)


def _sigmoid(x):
    return 1.0 / (1.0 + jnp.exp(-x))


def _rotate(x, cos, sin_up, sin_dn, shift):
    return (x * cos + pltpu.roll(x, LANES - shift, 1) * sin_up
            + pltpu.roll(x, shift, 1) * sin_dn)


IN_TM = 512
IN_TN = 512
_IN_OUT_WIDTHS = (3 * ATT_W, 2 * RET_QK_W, RET_V_W, RET_V_W, D_MODEL, D_MODEL)


def _in_proj_kernel(x_ref, g_ref, w_ref, *rest):
    out_refs, u_ref = rest[:-1], rest[-1]
    u_ref[...] = (_rms(x_ref[...]) * g_ref[...]).astype(BF16)
    off = 0
    for o_ref, width in zip(out_refs, _IN_OUT_WIDTHS):
        for c in range(0, width, IN_TN):
            o_ref[:, c:c + IN_TN] = jnp.dot(
                u_ref[...], w_ref[:, off + c:off + c + IN_TN],
                preferred_element_type=F32).astype(BF16)
        off += width


def _in_proj(h, gain, w):
    t = h.shape[0]
    return pl.pallas_call(
        _in_proj_kernel,
        grid=(t // IN_TM,),
        in_specs=[
            pl.BlockSpec((IN_TM, D_MODEL), lambda i: (i, 0)),
            pl.BlockSpec((1, D_MODEL), lambda i: (0, 0)),
            pl.BlockSpec((D_MODEL, IN_WIDTH), lambda i: (0, 0), pipeline_mode=pl.Buffered(1)),
        ],
        out_specs=[pl.BlockSpec((IN_TM, wd), lambda i: (i, 0)) for wd in _IN_OUT_WIDTHS],
        out_shape=[jax.ShapeDtypeStruct((t, wd), BF16) for wd in _IN_OUT_WIDTHS],
        scratch_shapes=[pltpu.VMEM((IN_TM, D_MODEL), BF16)],
        compiler_params=pltpu.CompilerParams(
            dimension_semantics=("arbitrary",), vmem_limit_bytes=VMEM_LIMIT),
        name="in_proj",
    )(h, gain, w)


ATT_PAD = HALF_WIDTH * max(DILATIONS)
PREP_ROWS = 512
ATT_Q_SCALE = ATT_HEAD_DIM ** -0.5 * 1.4426950408889634
ATT_UNROLL = 8


def _attn_kernel(q_ref, k_ref, v_ref, cos_ref, sup_ref, sdn_ref, o_ref,
                 qs, kp, vp, acc, mrun, lrun, band):
    s_len = q_ref.shape[0]
    zeros = jnp.zeros((ATT_PAD, LANES), F32)
    kp[0:ATT_PAD, :] = zeros
    kp[ATT_PAD + s_len:, :] = zeros
    vp[0:ATT_PAD, :] = zeros
    vp[ATT_PAD + s_len:, :] = zeros

    d = (lax.broadcasted_iota(jnp.int32, (QBLK, KBLK), 1)
         - lax.broadcasted_iota(jnp.int32, (QBLK, KBLK), 0))
    for n, lo in enumerate((0, -HALF_WIDTH, QBLK - HALF_WIDTH)):
        band[n] = jnp.where((d >= lo) & (d <= lo + 2 * HALF_WIDTH), 0.0, MASK_VALUE).astype(F32)

    def prep(j, carry):
        r0 = pl.multiple_of(j * PREP_ROWS, PREP_ROWS)
        rows = pl.ds(r0, PREP_ROWS)
        cos, sup, sdn = cos_ref[rows, :], sup_ref[rows, :], sdn_ref[rows, :]
        half = ATT_ROPE_DIM // 2
        q = _rotate(q_ref[rows, :].astype(F32), cos, sup, sdn, half)
        qs[rows, :] = q * ATT_Q_SCALE
        kp[pl.ds(ATT_PAD + r0, PREP_ROWS), :] = _rotate(k_ref[rows, :].astype(F32), cos, sup, sdn, half)
        vp[pl.ds(ATT_PAD + r0, PREP_ROWS), :] = v_ref[rows, :].astype(F32)
        return carry

    lax.fori_loop(0, s_len // PREP_ROWS, prep, 0)

    lane_q = lax.broadcasted_iota(jnp.int32, (QBLK, LANES), 1)
    head_q = (lane_q < ATT_HEAD_DIM, lane_q >= ATT_HEAD_DIM)
    key_j = lax.broadcasted_iota(jnp.int32, (1, KBLK), 1)

    def rows_of(start, size, r):
        return pl.ds(start, size) if r == 1 else pl.ds(start, size, stride=r)

    def block(q_rows, kb, v2, bias, first):
        qb = qs[q_rows, :]
        q2 = jnp.concatenate([jnp.where(head_q[0], qb, 0.0), jnp.where(head_q[1], qb, 0.0)],
                             axis=0).astype(BF16)
        s = lax.dot_general(q2, kb, (((1,), (1,)), ((), ())), preferred_element_type=F32)
        s = s + jnp.concatenate([bias, bias], axis=0)
        m = jnp.max(s, axis=-1, keepdims=True)
        p = jnp.exp2(s - m).astype(BF16)
        nd = jnp.dot(p, v2, preferred_element_type=F32)
        num = jnp.where(head_q[0], nd[:QBLK, :LANES], nd[QBLK:, :LANES])
        den = jnp.where(head_q[0], nd[:QBLK, LANES:], nd[QBLK:, LANES:])
        mx = jnp.where(head_q[0], m[:QBLK], m[QBLK:])
        if first:
            acc[q_rows, :] = num
            lrun[q_rows, :] = den
            mrun[q_rows, :] = mx
        else:
            m_old = mrun[q_rows, :]
            m_new = jnp.maximum(m_old, mx)
            a_old = jnp.exp2(m_old - m_new)
            a_new = jnp.exp2(mx - m_new)
            acc[q_rows, :] = a_old * acc[q_rows, :] + a_new * num
            lrun[q_rows, :] = a_old * lrun[q_rows, :] + a_new * den
            mrun[q_rows, :] = m_new

    def load_keys(k_rows):
        vb = vp[k_rows, :].astype(BF16)
        return kp[k_rows, :].astype(BF16), jnp.concatenate([vb, jnp.ones_like(vb)], axis=1)

    def run_group(r, first):
        sub_len = s_len // r
        log2r = r.bit_length() - 1

        def body(t, carry):
            c = jnp.bitwise_and(t, r - 1)
            i = lax.shift_right_logical(t, log2r)
            q_start = c + (r * QBLK) * i
            kb, v2 = load_keys(rows_of(ATT_PAD - r * HALF_WIDTH + q_start, KBLK, r))
            key_pos = QBLK * i - HALF_WIDTH + key_j
            bias = band[0] + jnp.where((key_pos >= 0) & (key_pos < sub_len), 0.0, MASK_VALUE)
            block(rows_of(q_start, QBLK, r), kb, v2, bias, first)
            return carry

        def whole(c, carry):
            kb, v2 = load_keys(rows_of(ATT_PAD + c, KBLK, r))
            for i in range(KBLK // QBLK):
                block(rows_of(c + (r * QBLK) * i, QBLK, r), kb, v2, band[1 + i], first)
            return carry

        if sub_len == KBLK:
            lax.fori_loop(0, r, whole, 0, unroll=ATT_UNROLL // 2)
        else:
            lax.fori_loop(0, s_len // QBLK, body, 0, unroll=ATT_UNROLL)

    for g, r in enumerate(sorted(DILATIONS, reverse=True)):
        run_group(r, g == 0)

    def finish(j, carry):
        rows = pl.ds(pl.multiple_of(j * PREP_ROWS, PREP_ROWS), PREP_ROWS)
        o_ref[rows, :] = (acc[rows, :] / lrun[rows, :]).astype(o_ref.dtype)
        return carry

    lax.fori_loop(0, s_len // PREP_ROWS, finish, 0)


def _attention(qkv, cos, sup, sdn):
    b, s, _ = qkv.shape
    pairs = ATT_W // LANES

    def col(section):
        return pl.BlockSpec((None, s, LANES), lambda bi, hp: (bi, 0, section * pairs + hp))

    table = pl.BlockSpec((s, LANES), lambda bi, hp: (0, 0), pipeline_mode=pl.Buffered(1))
    return pl.pallas_call(
        _attn_kernel,
        grid=(b, pairs),
        in_specs=[col(0), col(1), col(2), table, table, table],
        out_specs=pl.BlockSpec((None, s, LANES), lambda bi, hp: (bi, 0, hp)),
        out_shape=jax.ShapeDtypeStruct((b, s, ATT_W), BF16),
        scratch_shapes=[
            pltpu.VMEM((s, LANES), F32),
            pltpu.VMEM((s + 2 * ATT_PAD, LANES), F32),
            pltpu.VMEM((s + 2 * ATT_PAD, LANES), F32),
            pltpu.VMEM((s, LANES), F32),
            pltpu.VMEM((s, LANES), F32),
            pltpu.VMEM((s, LANES), F32),
            pltpu.VMEM((3, QBLK, KBLK), F32),
        ],
        compiler_params=pltpu.CompilerParams(
            dimension_semantics=("arbitrary", "arbitrary"), vmem_limit_bytes=VMEM_LIMIT),
        name="attention",
    )(qkv, qkv, qkv, cos, sup, sdn)


RET_UNROLL = 2


def _ret_kernel(q_ref, k_ref, v_ref, g_ref, dl_ref, cos_ref, sup_ref, sdn_ref, o_ref,
                qs, kt, states, sf, sb):
    s_len = q_ref.shape[0]
    c = RET_CHUNK
    n_chunks = s_len // c
    half = RET_QK_DIM // 2

    def prep(t, carry):
        rows = pl.ds(pl.multiple_of(t * c, c), c)
        cos, sup, sdn = cos_ref[rows, :], sup_ref[rows, :], sdn_ref[rows, :]
        qs[rows, :] = _rotate(q_ref[rows, :].astype(F32), cos, sup, sdn, half)
        k = _rotate(k_ref[rows, :].astype(F32), cos, sup, sdn, half) * (RET_QK_DIM ** -0.5)
        kt[t] = k.T
        return carry

    lax.fori_loop(0, n_chunks, prep, 0, unroll=2)

    x = dl_ref[...]
    log_gamma = jnp.minimum(x, 0.0) - jnp.log(1.0 + jnp.exp(-jnp.abs(x)))
    lgf, lgb = log_gamma[0:1, :], log_gamma[1:2, :]
    row_i = lax.broadcasted_iota(jnp.int32, (c, LANES), 0)
    lane_i = lax.broadcasted_iota(jnp.int32, (c, LANES), 1)
    row, lane = row_i.astype(F32), lane_i.astype(F32)
    head_lane = (lane_i < RET_QK_DIM, lane_i >= RET_QK_DIM)

    def per_row(lg):
        return jnp.where(row_i < RET_QK_DIM, lg[:, :LANES], lg[:, LANES:])

    def per_lane(lg):
        return jnp.where(head_lane[0], lg[:, :LANES], lg[:, LANES:])

    zeta_f = jnp.exp((c - 1.0 - lane) * per_row(lgf))
    zeta_b = jnp.exp(lane * per_row(lgb))
    xi_f = jnp.exp((row + 1.0) * per_lane(lgf))
    xi_b = jnp.exp((c - row) * per_lane(lgb))
    diff = row - lane
    decay = jnp.concatenate(
        [jnp.exp(jnp.where(diff >= 0, diff * lgf[:, h * LANES:(h + 1) * LANES],
                           -diff * lgb[:, h * LANES:(h + 1) * LANES])) for h in range(2)], axis=0)

    srow = lax.broadcasted_iota(jnp.int32, (LANES, 2 * LANES), 0)
    scol = lax.broadcasted_iota(jnp.int32, (LANES, 2 * LANES), 1)
    same_head = (srow < RET_QK_DIM) == (scol < LANES)
    gc_f = jnp.where(same_head, jnp.exp(c * lgf), 0.0)
    gc_b = jnp.where(same_head, jnp.exp(c * lgb), 0.0)

    sf[...] = jnp.zeros_like(sf)
    sb[...] = jnp.zeros_like(sb)

    def scan(t, carry):
        for state, zeta, gc, tt, half_rows in ((sf, zeta_f, gc_f, t, slice(0, LANES)),
                                               (sb, zeta_b, gc_b, n_chunks - 1 - t, slice(LANES, 2 * LANES))):
            rows = pl.ds(pl.multiple_of(tt * c, c), c)
            old = state[...]
            states[tt, half_rows, :] = old.astype(BF16)
            kv = jnp.dot((kt[tt] * zeta).astype(BF16), v_ref[rows, :], preferred_element_type=F32)
            state[...] = gc * old + jnp.where(same_head, kv, 0.0)
        return carry

    lax.fori_loop(0, n_chunks, scan, 0, unroll=RET_UNROLL)

    vrow = lax.broadcasted_iota(jnp.int32, (2 * c, 2 * LANES), 0)
    vcol = lax.broadcasted_iota(jnp.int32, (2 * c, 2 * LANES), 1)
    v_diag = (vrow < c) == (vcol < LANES)

    def combine(t, carry):
        rows = pl.ds(pl.multiple_of(t * c, c), c)
        q = qs[rows, :]
        q2 = jnp.concatenate([jnp.where(head_lane[0], q, 0.0), jnp.where(head_lane[1], q, 0.0)],
                             axis=0).astype(BF16)
        s = jnp.dot(q2, kt[t].astype(BF16), preferred_element_type=F32) * decay
        s_wide = jnp.concatenate([s[:c], s[c:]], axis=1).astype(BF16)
        v = v_ref[rows, :]
        v_blocks = jnp.where(v_diag, jnp.concatenate([v, v], axis=0), 0.0)
        y = jnp.dot(s_wide, v_blocks, preferred_element_type=F32)
        q_cross = jnp.concatenate([q * xi_f, q * xi_b], axis=1).astype(BF16)
        y = y + jnp.dot(q_cross, states[t], preferred_element_type=F32)
        for hh in range(2):
            cols = slice(hh * LANES, (hh + 1) * LANES)
            yh = y[:, cols]
            mu = jnp.mean(yh, axis=-1, keepdims=True)
            yc = yh - mu
            var = jnp.mean(yc * yc, axis=-1, keepdims=True)
            yn = yc * lax.rsqrt(var + GN_EPS)
            g = g_ref[rows, cols].astype(F32)
            o_ref[rows, cols] = (g * _sigmoid(g) * yn).astype(o_ref.dtype)
        return carry

    lax.fori_loop(0, n_chunks, combine, 0, unroll=2 * RET_UNROLL)


def _retention(qk, v, g, decay_lanes, cos, sup, sdn):
    b, s, _ = qk.shape
    pairs = RET_QK_W // LANES
    n_chunks = s // RET_CHUNK
    table = pl.BlockSpec((s, LANES), lambda bi, hp: (0, 0), pipeline_mode=pl.Buffered(1))
    wide = pl.BlockSpec((None, s, 2 * LANES), lambda bi, hp: (bi, 0, hp))
    return pl.pallas_call(
        _ret_kernel,
        grid=(b, pairs),
        in_specs=[
            pl.BlockSpec((None, s, LANES), lambda bi, hp: (bi, 0, hp)),
            pl.BlockSpec((None, s, LANES), lambda bi, hp: (bi, 0, pairs + hp)),
            wide, wide,
            pl.BlockSpec((2, 2 * LANES), lambda bi, hp: (0, hp)),
            table, table, table,
        ],
        out_specs=wide,
        out_shape=jax.ShapeDtypeStruct((b, s, RET_V_W), BF16),
        scratch_shapes=[
            pltpu.VMEM((s, LANES), F32),
            pltpu.VMEM((n_chunks, LANES, RET_CHUNK), F32),
            pltpu.VMEM((n_chunks, 2 * LANES, 2 * LANES), BF16),
            pltpu.VMEM((LANES, 2 * LANES), F32),
            pltpu.VMEM((LANES, 2 * LANES), F32),
        ],
        compiler_params=pltpu.CompilerParams(
            dimension_semantics=("arbitrary", "arbitrary"), vmem_limit_bytes=VMEM_LIMIT),
        name="retention",
    )(qk, qk, v, g, decay_lanes, cos, sup, sdn)


POST_TM = 256
FF_CHUNK = 1024


def _post_kernel(h_ref, att_ref, ret_ref, ga_ref, gb_ref, p_ref,
                 w_att_ref, w_ret_ref, w_out_ref, w_up_ref, w_down_ref, w_gate_ref, w_proj_ref,
                 n_post_ref, n_mlp_pre_ref, n_mlp_post_ref, n_ple_ref, o_ref, u_ref):
    a = jnp.dot(att_ref[...], w_att_ref[...], preferred_element_type=F32)
    r = jnp.dot(ret_ref[...], w_ret_ref[...], preferred_element_type=F32)
    merged = (_sigmoid(ga_ref[...].astype(F32)) * a + _sigmoid(gb_ref[...].astype(F32)) * r)
    y = jnp.dot(merged.astype(BF16), w_out_ref[...], preferred_element_type=F32)
    h = h_ref[...] + _rms(y) * n_post_ref[...]

    u_ref[...] = (_rms(h) * n_mlp_pre_ref[...]).astype(BF16)
    ff = jnp.zeros((POST_TM, D_MODEL), F32)
    for c in range(0, D_FF, FF_CHUNK):
        z = jnp.dot(u_ref[...], w_up_ref[:, c:c + FF_CHUNK], preferred_element_type=F32)
        z = jnp.maximum(z, 0.0)
        ff = ff + jnp.dot((z * z).astype(BF16), w_down_ref[c:c + FF_CHUNK, :],
                          preferred_element_type=F32)
    h = h + _rms(ff) * n_mlp_post_ref[...]

    gate = _sigmoid(jnp.dot(_rms(h).astype(BF16), w_gate_ref[...], preferred_element_type=F32))
    pe = jnp.dot(p_ref[...].astype(BF16), w_proj_ref[...], preferred_element_type=F32)
    o_ref[...] = h + gate * (_rms(pe) * n_ple_ref[...])


def _post(h, att, ret, ga, gb, p, w_att, w_ret, w_out, w_up, w_down, w_gate, w_proj,
          n_post, n_mlp_pre, n_mlp_post, n_ple):
    t = h.shape[0]

    def rows(width):
        return pl.BlockSpec((POST_TM, width), lambda i: (i, 0))

    def whole(arr):
        return pl.BlockSpec(arr.shape, lambda i: (0, 0), pipeline_mode=pl.Buffered(1))

    weights = (w_att, w_ret, w_out, w_up, w_down, w_gate, w_proj)
    gains = (n_post, n_mlp_pre, n_mlp_post, n_ple)
    return pl.pallas_call(
        _post_kernel,
        grid=(t // POST_TM,),
        in_specs=[rows(D_MODEL), rows(ATT_W), rows(RET_V_W), rows(D_MODEL), rows(D_MODEL),
                  rows(PLE_DIM)] + [whole(w) for w in weights] + [whole(g) for g in gains],
        out_specs=rows(D_MODEL),
        out_shape=jax.ShapeDtypeStruct((t, D_MODEL), F32),
        scratch_shapes=[pltpu.VMEM((POST_TM, D_MODEL), BF16)],
        compiler_params=pltpu.CompilerParams(
            dimension_semantics=("arbitrary",), vmem_limit_bytes=VMEM_LIMIT),
        name="post",
    )(h, att, ret, ga, gb, p, *weights, *gains)


def _rotary_tables(s, rot_dim, head_dim, theta):
    half = rot_dim // 2
    inv = theta ** (-jnp.arange(half, dtype=F32) * 2.0 / rot_dim)
    ang = jnp.arange(s).astype(F32)[:, None] * inv[None, :]
    within = jnp.arange(LANES) % head_dim
    cos = jnp.cos(ang)[:, within % half]
    sin = jnp.sin(ang)[:, within % half]
    rotated = within < rot_dim
    lower = within < half
    cos_t = jnp.where(rotated, cos, 1.0)
    sin_up = jnp.where(lower, -sin, 0.0)
    sin_dn = jnp.where(rotated & ~lower, sin, 0.0)
    return cos_t, sin_up, sin_dn


def kernel(x, p, w_in, w_att_out, w_ret_out, w_out, w_mlp_up, w_mlp_down, w_ple_gate, w_ple_proj,
           ret_decay_logit, norm_mix_pre, norm_mix_post, norm_mlp_pre, norm_mlp_post, norm_ple):
    b, s, d = x.shape
    depth = w_in.shape[0]
    t = b * s
    att_tables = _rotary_tables(s, ATT_ROPE_DIM, ATT_HEAD_DIM, ATT_ROPE_THETA)
    ret_tables = _rotary_tables(s, RET_QK_DIM, RET_QK_DIM, RET_THETA)

    h = x.reshape(t, d)
    for i in range(depth):
        def gain(a):
            return a[i].reshape(1, d)

        qkv, rqk, rv, rg, ga, gb = _in_proj(h, gain(norm_mix_pre), w_in[i].astype(BF16))
        att = _attention(qkv.reshape(b, s, -1), *att_tables)
        decay_lanes = jnp.repeat(ret_decay_logit[i], RET_V_DIM, axis=-1)
        ret = _retention(rqk.reshape(b, s, -1), rv.reshape(b, s, -1), rg.reshape(b, s, -1),
                         decay_lanes, *ret_tables)
        h = _post(h, att.reshape(t, -1), ret.reshape(t, -1), ga, gb, p[i].reshape(t, -1),
                  w_att_out[i].astype(BF16), w_ret_out[i].astype(BF16), w_out[i].astype(BF16),
                  w_mlp_up[i].astype(BF16), w_mlp_down[i].astype(BF16),
                  w_ple_gate[i].astype(BF16), w_ple_proj[i].astype(BF16),
                  gain(norm_mix_post), gain(norm_mlp_pre), gain(norm_mlp_post), gain(norm_ple))
    return h.reshape(b, s, d)
```

```python
import functools

import jax
import jax.numpy as jnp
from jax import lax
from jax.experimental import pallas as pl
from jax.experimental.pallas import tpu as pltpu

D_MODEL = 1024
ATT_HEADS = 8
ATT_HEAD_DIM = 64
ATT_ROPE_DIM = 16
ATT_ROPE_THETA = 500000.0
DILATIONS = (1, 4, 16)
HALF_WIDTH = 64
RET_HEADS = 8
RET_QK_DIM = 64
RET_V_DIM = 128
RET_THETA = 10000.0
RET_CHUNK = 128
D_FF = 4096
PLE_DIM = 256
NORM_EPS = 1e-6
GN_EPS = 1e-5
MASK_VALUE = -1e30

ATT_W = ATT_HEADS * ATT_HEAD_DIM
RET_QK_W = RET_HEADS * RET_QK_DIM
RET_V_W = RET_HEADS * RET_V_DIM
IN_WIDTH = 3 * ATT_W + 2 * RET_QK_W + 2 * RET_V_W + 2 * D_MODEL

LANES = 128
QBLK = 128
KBLK = QBLK + 2 * HALF_WIDTH
VMEM_LIMIT = 56 * 1024 * 1024

F32 = jnp.float32
BF16 = jnp.bfloat16


def _rms(x):
    return x * lax.rsqrt(jnp.mean(x * x, axis=-1, keepdims=True) + NORM_EPS)


def _sigmoid(x):
    return 1.0 / (1.0 + jnp.exp(-x))


def _rotate(x, cos, sin_up, sin_dn, shift):
    return (x * cos + pltpu.roll(x, LANES - shift, 1) * sin_up
            + pltpu.roll(x, shift, 1) * sin_dn)


IN_TM = 512
IN_TN = 512
_IN_OUT_WIDTHS = (3 * ATT_W, 2 * RET_QK_W, RET_V_W, RET_V_W, D_MODEL, D_MODEL)


def _in_proj_kernel(x_ref, g_ref, w_ref, ac_ref, au_ref, ad_ref, rc_ref, ru_ref, rd_ref, *rest):
    out_refs, u_ref = rest[:-1], rest[-1]
    u_ref[...] = (_rms(x_ref[...]) * g_ref[...]).astype(BF16)

    def rotary(tables, half):
        def apply(y):
            cos, sup, sdn = (tb[...] for tb in tables)
            return jnp.concatenate(
                [_rotate(y[:, j:j + LANES], cos, sup, sdn, half) for j in range(0, IN_TN, LANES)], axis=1)
        return apply

    att_rot = rotary((ac_ref, au_ref, ad_ref), ATT_ROPE_DIM // 2)
    ret_rot = rotary((rc_ref, ru_ref, rd_ref), RET_QK_DIM // 2)
    epilogues = {(0, 0): lambda y: att_rot(y) * ATT_Q_SCALE, (0, ATT_W): att_rot,
                 (1, 0): ret_rot, (1, RET_QK_W): lambda y: ret_rot(y) * (RET_QK_DIM ** -0.5),
                 (3, 0): lambda y: y * _sigmoid(y), (3, IN_TN): lambda y: y * _sigmoid(y)}
    off = 0
    for n, (o_ref, width) in enumerate(zip(out_refs, _IN_OUT_WIDTHS)):
        for c in range(0, width, IN_TN):
            y = jnp.dot(u_ref[...], w_ref[:, off + c:off + c + IN_TN], preferred_element_type=F32)
            o_ref[:, c:c + IN_TN] = epilogues.get((n, c), lambda y: y)(y).astype(BF16)
        off += width


def _in_proj(h, gain, w, att_tables, ret_tables, s_len):
    t = h.shape[0]
    table = pl.BlockSpec((IN_TM, LANES), lambda i: (i % (s_len // IN_TM), 0))
    return pl.pallas_call(
        _in_proj_kernel,
        grid=(t // IN_TM,),
        in_specs=[
            pl.BlockSpec((IN_TM, D_MODEL), lambda i: (i, 0)),
            pl.BlockSpec((1, D_MODEL), lambda i: (0, 0)),
            pl.BlockSpec((D_MODEL, IN_WIDTH), lambda i: (0, 0), pipeline_mode=pl.Buffered(1)),
        ] + [table] * 6,
        out_specs=[pl.BlockSpec((IN_TM, wd), lambda i: (i, 0)) for wd in _IN_OUT_WIDTHS],
        out_shape=[jax.ShapeDtypeStruct((t, wd), BF16) for wd in _IN_OUT_WIDTHS],
        scratch_shapes=[pltpu.VMEM((IN_TM, D_MODEL), BF16)],
        compiler_params=pltpu.CompilerParams(
            dimension_semantics=("arbitrary",), vmem_limit_bytes=VMEM_LIMIT),
        name="in_proj",
    )(h, gain, w, *att_tables, *ret_tables)


ATT_PAD = HALF_WIDTH * max(DILATIONS)
PREP_ROWS = 512
ATT_Q_SCALE = ATT_HEAD_DIM ** -0.5 * 1.4426950408889634
ATT_UNROLL = 8


def _attn_kernel(q_ref, k_ref, v_ref, o_ref, qs, kp, vp, acc, mrun, lrun, band):
    s_len = q_ref.shape[0]
    zeros = jnp.zeros((ATT_PAD, LANES), F32)
    kp[0:ATT_PAD, :] = zeros
    kp[ATT_PAD + s_len:, :] = zeros
    vp[0:ATT_PAD, :] = zeros
    vp[ATT_PAD + s_len:, :] = zeros

    d = (lax.broadcasted_iota(jnp.int32, (QBLK, KBLK), 1)
         - lax.broadcasted_iota(jnp.int32, (QBLK, KBLK), 0))
    for n, lo in enumerate((0, -HALF_WIDTH, QBLK - HALF_WIDTH)):
        band[n] = jnp.where((d >= lo) & (d <= lo + 2 * HALF_WIDTH), 0.0, MASK_VALUE).astype(F32)

    def prep(j, carry):
        r0 = pl.multiple_of(j * PREP_ROWS, PREP_ROWS)
        rows = pl.ds(r0, PREP_ROWS)
        qs[rows, :] = q_ref[rows, :].astype(F32)
        kp[pl.ds(ATT_PAD + r0, PREP_ROWS), :] = k_ref[rows, :].astype(F32)
        vp[pl.ds(ATT_PAD + r0, PREP_ROWS), :] = v_ref[rows, :].astype(F32)
        return carry

    lax.fori_loop(0, s_len // PREP_ROWS, prep, 0)

    lane_q = lax.broadcasted_iota(jnp.int32, (QBLK, LANES), 1)
    head_q = (lane_q < ATT_HEAD_DIM, lane_q >= ATT_HEAD_DIM)
    key_j = lax.broadcasted_iota(jnp.int32, (1, KBLK), 1)

    def rows_of(start, size, r):
        return pl.ds(start, size) if r == 1 else pl.ds(start, size, stride=r)

    def block(q_rows, kb, v2, bias, first):
        qb = qs[q_rows, :]
        q2 = jnp.concatenate([jnp.where(head_q[0], qb, 0.0), jnp.where(head_q[1], qb, 0.0)],
                             axis=0).astype(BF16)
        s = lax.dot_general(q2, kb, (((1,), (1,)), ((), ())), preferred_element_type=F32)
        s = s + jnp.concatenate([bias, bias], axis=0)
        m = jnp.max(s, axis=-1, keepdims=True)
        p = jnp.exp2(s - m).astype(BF16)
        nd = jnp.dot(p, v2, preferred_element_type=F32)
        num = jnp.where(head_q[0], nd[:QBLK, :LANES], nd[QBLK:, :LANES])
        den = jnp.where(head_q[0], nd[:QBLK, LANES:], nd[QBLK:, LANES:])
        mx = jnp.where(head_q[0], m[:QBLK], m[QBLK:])
        if first:
            acc[q_rows, :] = num
            lrun[q_rows, :] = den
            mrun[q_rows, :] = mx
        else:
            m_old = mrun[q_rows, :]
            m_new = jnp.maximum(m_old, mx)
            a_old = jnp.exp2(m_old - m_new)
            a_new = jnp.exp2(mx - m_new)
            acc[q_rows, :] = a_old * acc[q_rows, :] + a_new * num
            lrun[q_rows, :] = a_old * lrun[q_rows, :] + a_new * den
            mrun[q_rows, :] = m_new

    def load_keys(k_rows):
        vb = vp[k_rows, :].astype(BF16)
        return kp[k_rows, :].astype(BF16), jnp.concatenate([vb, jnp.ones_like(vb)], axis=1)

    def run_group(r, first):
        sub_len = s_len // r
        log2r = r.bit_length() - 1

        def body(t, carry):
            c = jnp.bitwise_and(t, r - 1)
            i = lax.shift_right_logical(t, log2r)
            q_start = c + (r * QBLK) * i
            kb, v2 = load_keys(rows_of(ATT_PAD - r * HALF_WIDTH + q_start, KBLK, r))
            key_pos = QBLK * i - HALF_WIDTH + key_j
            bias = band[0] + jnp.where((key_pos >= 0) & (key_pos < sub_len), 0.0, MASK_VALUE)
            block(rows_of(q_start, QBLK, r), kb, v2, bias, first)
            return carry

        def whole(c, carry):
            kb, v2 = load_keys(rows_of(ATT_PAD + c, KBLK, r))
            for i in range(KBLK // QBLK):
                block(rows_of(c + (r * QBLK) * i, QBLK, r), kb, v2, band[1 + i], first)
            return carry

        if sub_len == KBLK:
            lax.fori_loop(0, r, whole, 0, unroll=ATT_UNROLL // 2)
        else:
            lax.fori_loop(0, s_len // QBLK, body, 0, unroll=ATT_UNROLL)

    for g, r in enumerate(sorted(DILATIONS, reverse=True)):
        run_group(r, g == 0)

    def finish(j, carry):
        rows = pl.ds(pl.multiple_of(j * PREP_ROWS, PREP_ROWS), PREP_ROWS)
        o_ref[rows, :] = (acc[rows, :] / lrun[rows, :]).astype(o_ref.dtype)
        return carry

    lax.fori_loop(0, s_len // PREP_ROWS, finish, 0)


def _attention(qkv):
    b, s, _ = qkv.shape
    pairs = ATT_W // LANES

    def col(section):
        return pl.BlockSpec((None, s, LANES), lambda bi, hp: (bi, 0, section * pairs + hp))

    return pl.pallas_call(
        _attn_kernel,
        grid=(b, pairs),
        in_specs=[col(0), col(1), col(2)],
        out_specs=pl.BlockSpec((None, s, LANES), lambda bi, hp: (bi, 0, hp)),
        out_shape=jax.ShapeDtypeStruct((b, s, ATT_W), BF16),
        scratch_shapes=[
            pltpu.VMEM((s, LANES), F32),
            pltpu.VMEM((s + 2 * ATT_PAD, LANES), F32),
            pltpu.VMEM((s + 2 * ATT_PAD, LANES), F32),
            pltpu.VMEM((s, LANES), F32),
            pltpu.VMEM((s, LANES), F32),
            pltpu.VMEM((s, LANES), F32),
            pltpu.VMEM((3, QBLK, KBLK), F32),
        ],
        compiler_params=pltpu.CompilerParams(
            dimension_semantics=("arbitrary", "arbitrary"), vmem_limit_bytes=VMEM_LIMIT),
        name="attention",
    )(qkv, qkv, qkv)


RET_UNROLL = 2


def _ret_kernel(q_ref, k_ref, v_ref, g_ref, dl_ref, o_ref, kt, states, sf, sb):
    s_len = q_ref.shape[0]
    c = RET_CHUNK
    n_chunks = s_len // c

    def prep(t, carry):
        rows = pl.ds(pl.multiple_of(t * c, c), c)
        kt[t] = k_ref[rows, :].astype(F32).T
        return carry

    lax.fori_loop(0, n_chunks, prep, 0, unroll=4)

    x = dl_ref[...]
    log_gamma = jnp.minimum(x, 0.0) - jnp.log(1.0 + jnp.exp(-jnp.abs(x)))
    lgf, lgb = log_gamma[0:1, :], log_gamma[1:2, :]
    row_i = lax.broadcasted_iota(jnp.int32, (c, LANES), 0)
    lane_i = lax.broadcasted_iota(jnp.int32, (c, LANES), 1)
    row, lane = row_i.astype(F32), lane_i.astype(F32)
    head_lane = (lane_i < RET_QK_DIM, lane_i >= RET_QK_DIM)

    def per_row(lg):
        return jnp.where(row_i < RET_QK_DIM, lg[:, :LANES], lg[:, LANES:])

    def per_lane(lg):
        return jnp.where(head_lane[0], lg[:, :LANES], lg[:, LANES:])

    zeta_f = jnp.exp((c - 1.0 - lane) * per_row(lgf))
    zeta_b = jnp.exp(lane * per_row(lgb))
    xi_f = jnp.exp((row + 1.0) * per_lane(lgf))
    xi_b = jnp.exp((c - row) * per_lane(lgb))
    diff = row - lane
    decay = jnp.concatenate(
        [jnp.exp(jnp.where(diff >= 0, diff * lgf[:, h * LANES:(h + 1) * LANES],
                           -diff * lgb[:, h * LANES:(h + 1) * LANES])) for h in range(2)], axis=0)

    srow = lax.broadcasted_iota(jnp.int32, (LANES, 2 * LANES), 0)
    scol = lax.broadcasted_iota(jnp.int32, (LANES, 2 * LANES), 1)
    same_head = (srow < RET_QK_DIM) == (scol < LANES)
    gc_f = jnp.where(same_head, jnp.exp(c * lgf), 0.0)
    gc_b = jnp.where(same_head, jnp.exp(c * lgb), 0.0)

    sf[...] = jnp.zeros_like(sf)
    sb[...] = jnp.zeros_like(sb)

    def scan(t, carry):
        for state, zeta, gc, tt, half_rows in ((sf, zeta_f, gc_f, t, slice(0, LANES)),
                                               (sb, zeta_b, gc_b, n_chunks - 1 - t, slice(LANES, 2 * LANES))):
            rows = pl.ds(pl.multiple_of(tt * c, c), c)
            old = state[...]
            states[tt, half_rows, :] = old.astype(BF16)
            kv = jnp.dot((kt[tt] * zeta).astype(BF16), v_ref[rows, :], preferred_element_type=F32)
            state[...] = gc * old + jnp.where(same_head, kv, 0.0)
        return carry

    lax.fori_loop(0, n_chunks, scan, 0, unroll=RET_UNROLL)

    vrow = lax.broadcasted_iota(jnp.int32, (2 * c, 2 * LANES), 0)
    vcol = lax.broadcasted_iota(jnp.int32, (2 * c, 2 * LANES), 1)
    v_diag = (vrow < c) == (vcol < LANES)

    def combine(t, carry):
        rows = pl.ds(pl.multiple_of(t * c, c), c)
        q = q_ref[rows, :].astype(F32)
        q2 = jnp.concatenate([jnp.where(head_lane[0], q, 0.0), jnp.where(head_lane[1], q, 0.0)],
                             axis=0).astype(BF16)
        s = jnp.dot(q2, kt[t].astype(BF16), preferred_element_type=F32) * decay
        s_wide = jnp.concatenate([s[:c], s[c:]], axis=1).astype(BF16)
        v = v_ref[rows, :]
        v_blocks = jnp.where(v_diag, jnp.concatenate([v, v], axis=0), 0.0)
        y = jnp.dot(s_wide, v_blocks, preferred_element_type=F32)
        q_cross = jnp.concatenate([q * xi_f, q * xi_b], axis=1).astype(BF16)
        y = y + jnp.dot(q_cross, states[t], preferred_element_type=F32)
        for hh in range(2):
            cols = slice(hh * LANES, (hh + 1) * LANES)
            yh = y[:, cols]
            mu = jnp.mean(yh, axis=-1, keepdims=True)
            yc = yh - mu
            var = jnp.mean(yc * yc, axis=-1, keepdims=True)
            yn = yc * lax.rsqrt(var + GN_EPS)
            o_ref[rows, cols] = (g_ref[rows, cols].astype(F32) * yn).astype(o_ref.dtype)
        return carry

    lax.fori_loop(0, n_chunks, combine, 0, unroll=2 * RET_UNROLL)


def _retention(qk, v, g, decay_lanes):
    b, s, _ = qk.shape
    pairs = RET_QK_W // LANES
    n_chunks = s // RET_CHUNK
    wide = pl.BlockSpec((None, s, 2 * LANES), lambda bi, hp: (bi, 0, hp))
    return pl.pallas_call(
        _ret_kernel,
        grid=(b, pairs),
        in_specs=[
            pl.BlockSpec((None, s, LANES), lambda bi, hp: (bi, 0, hp)),
            pl.BlockSpec((None, s, LANES), lambda bi, hp: (bi, 0, pairs + hp)),
            wide, wide,
            pl.BlockSpec((2, 2 * LANES), lambda bi, hp: (0, hp)),
        ],
        out_specs=wide,
        out_shape=jax.ShapeDtypeStruct((b, s, RET_V_W), BF16),
        scratch_shapes=[
            pltpu.VMEM((n_chunks, LANES, RET_CHUNK), F32),
            pltpu.VMEM((n_chunks, 2 * LANES, 2 * LANES), BF16),
            pltpu.VMEM((LANES, 2 * LANES), F32),
            pltpu.VMEM((LANES, 2 * LANES), F32),
        ],
        compiler_params=pltpu.CompilerParams(
            dimension_semantics=("arbitrary", "arbitrary"), vmem_limit_bytes=VMEM_LIMIT),
        name="retention",
    )(qk, qk, v, g, decay_lanes)


POST_TM = 512
FF_CHUNK = 1024


def _post_kernel(h_ref, att_ref, ret_ref, ga_ref, gb_ref, p_ref,
                 w_att_ref, w_ret_ref, w_out_ref, w_up_ref, w_down_ref, w_gate_ref, w_proj_ref,
                 n_post_ref, n_mlp_pre_ref, n_mlp_post_ref, n_ple_ref, o_ref, u_ref):
    a = jnp.dot(att_ref[...], w_att_ref[...], preferred_element_type=F32)
    r = jnp.dot(ret_ref[...], w_ret_ref[...], preferred_element_type=F32)
    merged = (_sigmoid(ga_ref[...].astype(F32)) * a + _sigmoid(gb_ref[...].astype(F32)) * r)
    y = jnp.dot(merged.astype(BF16), w_out_ref[...], preferred_element_type=F32)
    h = h_ref[...] + _rms(y) * n_post_ref[...]

    u_ref[...] = (_rms(h) * n_mlp_pre_ref[...]).astype(BF16)
    ff = jnp.zeros((POST_TM, D_MODEL), F32)
    for c in range(0, D_FF, FF_CHUNK):
        z = jnp.dot(u_ref[...], w_up_ref[:, c:c + FF_CHUNK], preferred_element_type=F32)
        z = jnp.maximum(z, 0.0)
        ff = ff + jnp.dot((z * z).astype(BF16), w_down_ref[c:c + FF_CHUNK, :],
                          preferred_element_type=F32)
    h = h + _rms(ff) * n_mlp_post_ref[...]

    gate = _sigmoid(jnp.dot(_rms(h).astype(BF16), w_gate_ref[...], preferred_element_type=F32))
    pe = jnp.dot(p_ref[...].astype(BF16), w_proj_ref[...], preferred_element_type=F32)
    o_ref[...] = h + gate * (_rms(pe) * n_ple_ref[...])


def _post(h, att, ret, ga, gb, p, w_att, w_ret, w_out, w_up, w_down, w_gate, w_proj,
          n_post, n_mlp_pre, n_mlp_post, n_ple):
    t = h.shape[0]

    def rows(width):
        return pl.BlockSpec((POST_TM, width), lambda i: (i, 0))

    def whole(arr):
        return pl.BlockSpec(arr.shape, lambda i: (0, 0), pipeline_mode=pl.Buffered(1))

    weights = (w_att, w_ret, w_out, w_up, w_down, w_gate, w_proj)
    gains = (n_post, n_mlp_pre, n_mlp_post, n_ple)
    return pl.pallas_call(
        _post_kernel,
        grid=(t // POST_TM,),
        in_specs=[rows(D_MODEL), rows(ATT_W), rows(RET_V_W), rows(D_MODEL), rows(D_MODEL),
                  rows(PLE_DIM)] + [whole(w) for w in weights] + [whole(g) for g in gains],
        out_specs=rows(D_MODEL),
        out_shape=jax.ShapeDtypeStruct((t, D_MODEL), F32),
        scratch_shapes=[pltpu.VMEM((POST_TM, D_MODEL), BF16)],
        compiler_params=pltpu.CompilerParams(
            dimension_semantics=("arbitrary",), vmem_limit_bytes=VMEM_LIMIT),
        name="post",
    )(h, att, ret, ga, gb, p, *weights, *gains)


def _rotary_tables(s, rot_dim, head_dim, theta):
    half = rot_dim // 2
    inv = theta ** (-jnp.arange(half, dtype=F32) * 2.0 / rot_dim)
    ang = jnp.arange(s).astype(F32)[:, None] * inv[None, :]
    within = jnp.arange(LANES) % head_dim
    cos = jnp.cos(ang)[:, within % half]
    sin = jnp.sin(ang)[:, within % half]
    rotated = within < rot_dim
    lower = within < half
    cos_t = jnp.where(rotated, cos, 1.0)
    sin_up = jnp.where(lower, -sin, 0.0)
    sin_dn = jnp.where(rotated & ~lower, sin, 0.0)
    return cos_t, sin_up, sin_dn


def kernel(x, p, w_in, w_att_out, w_ret_out, w_out, w_mlp_up, w_mlp_down, w_ple_gate, w_ple_proj,
           ret_decay_logit, norm_mix_pre, norm_mix_post, norm_mlp_pre, norm_mlp_post, norm_ple):
    b, s, d = x.shape
    depth = w_in.shape[0]
    t = b * s
    att_tables = _rotary_tables(s, ATT_ROPE_DIM, ATT_HEAD_DIM, ATT_ROPE_THETA)
    ret_tables = _rotary_tables(s, RET_QK_DIM, RET_QK_DIM, RET_THETA)

    h = x.reshape(t, d)
    for i in range(depth):
        def gain(a):
            return a[i].reshape(1, d)

        qkv, rqk, rv, rg, ga, gb = _in_proj(h, gain(norm_mix_pre), w_in[i].astype(BF16),
                                            att_tables, ret_tables, s)
        att = _attention(qkv.reshape(b, s, -1))
        decay_lanes = jnp.repeat(ret_decay_logit[i], RET_V_DIM, axis=-1)
        ret = _retention(rqk.reshape(b, s, -1), rv.reshape(b, s, -1), rg.reshape(b, s, -1),
                         decay_lanes)
        h = _post(h, att.reshape(t, -1), ret.reshape(t, -1), ga, gb, p[i].reshape(t, -1),
                  w_att_out[i].astype(BF16), w_ret_out[i].astype(BF16), w_out[i].astype(BF16),
                  w_mlp_up[i].astype(BF16), w_mlp_down[i].astype(BF16),
                  w_ple_gate[i].astype(BF16), w_ple_proj[i].astype(BF16),
                  gain(norm_mix_post), gain(norm_mlp_pre), gain(norm_mlp_post), gain(norm_ple))
    return h.reshape(b, s, d)
```

```python
import functools

import jax
import jax.numpy as jnp
from jax import lax
from jax.experimental import pallas as pl
from jax.experimental.pallas import tpu as pltpu

D_MODEL = 1024
ATT_HEADS = 8
ATT_HEAD_DIM = 64
ATT_ROPE_DIM = 16
ATT_ROPE_THETA = 500000.0
DILATIONS = (1, 4, 16)
HALF_WIDTH = 64
RET_HEADS = 8
RET_QK_DIM = 64
RET_V_DIM = 128
RET_THETA = 10000.0
RET_CHUNK = 128
D_FF = 4096
PLE_DIM = 256
NORM_EPS = 1e-6
GN_EPS = 1e-5
MASK_VALUE = -1e30

ATT_W = ATT_HEADS * ATT_HEAD_DIM
RET_QK_W = RET_HEADS * RET_QK_DIM
RET_V_W = RET_HEADS * RET_V_DIM
IN_WIDTH = 3 * ATT_W + 2 * RET_QK_W + 2 * RET_V_W + 2 * D_MODEL

LANES = 128
QBLK = 128
KBLK = QBLK + 2 * HALF_WIDTH
VMEM_LIMIT = 56 * 1024 * 1024

F32 = jnp.float32
BF16 = jnp.bfloat16


def _rms(x):
    return x * lax.rsqrt(jnp.mean(x * x, axis=-1, keepdims=True) + NORM_EPS)


def _sigmoid(x):
    return 1.0 / (1.0 + jnp.exp(-x))


def _rotate(x, cos, sin_up, sin_dn, shift):
    return (x * cos + pltpu.roll(x, LANES - shift, 1) * sin_up
            + pltpu.roll(x, shift, 1) * sin_dn)


IN_TM = 512
IN_TN = 512
_IN_OUT_WIDTHS = (3 * ATT_W, 2 * RET_QK_W, RET_V_W, RET_V_W, D_MODEL, D_MODEL)


def _in_proj_kernel(x_ref, g_ref, w_ref, ac_ref, au_ref, ad_ref, rc_ref, ru_ref, rd_ref, *rest):
    out_refs, u_ref = rest[:-1], rest[-1]
    u_ref[...] = (_rms(x_ref[...]) * g_ref[...]).astype(BF16)

    def rotary(tables, half):
        def apply(y):
            cos, sup, sdn = (tb[...] for tb in tables)
            return jnp.concatenate(
                [_rotate(y[:, j:j + LANES], cos, sup, sdn, half) for j in range(0, IN_TN, LANES)], axis=1)
        return apply

    att_rot = rotary((ac_ref, au_ref, ad_ref), ATT_ROPE_DIM // 2)
    ret_rot = rotary((rc_ref, ru_ref, rd_ref), RET_QK_DIM // 2)
    epilogues = {(0, 0): lambda y: att_rot(y) * ATT_Q_SCALE, (0, ATT_W): att_rot,
                 (1, 0): ret_rot, (1, RET_QK_W): lambda y: ret_rot(y) * (RET_QK_DIM ** -0.5),
                 (3, 0): lambda y: y * _sigmoid(y), (3, IN_TN): lambda y: y * _sigmoid(y)}
    off = 0
    for n, (o_ref, width) in enumerate(zip(out_refs, _IN_OUT_WIDTHS)):
        for c in range(0, width, IN_TN):
            y = jnp.dot(u_ref[...], w_ref[:, off + c:off + c + IN_TN], preferred_element_type=F32)
            o_ref[:, c:c + IN_TN] = epilogues.get((n, c), lambda y: y)(y).astype(BF16)
        off += width


def _in_proj(layer, h, gain, w, att_tables, ret_tables, s_len):
    t = h.shape[0]
    table = pl.BlockSpec((IN_TM, LANES), lambda i: (i % (s_len // IN_TM), 0))
    return pl.pallas_call(
        _in_proj_kernel,
        grid=(t // IN_TM,),
        in_specs=[
            pl.BlockSpec((IN_TM, D_MODEL), lambda i: (i, 0)),
            pl.BlockSpec((None, 1, D_MODEL), lambda i: (layer, 0, 0)),
            pl.BlockSpec((None, D_MODEL, IN_WIDTH), lambda i: (layer, 0, 0), pipeline_mode=pl.Buffered(1)),
        ] + [table] * 6,
        out_specs=[pl.BlockSpec((IN_TM, wd), lambda i: (i, 0)) for wd in _IN_OUT_WIDTHS],
        out_shape=[jax.ShapeDtypeStruct((t, wd), BF16) for wd in _IN_OUT_WIDTHS],
        scratch_shapes=[pltpu.VMEM((IN_TM, D_MODEL), BF16)],
        compiler_params=pltpu.CompilerParams(
            dimension_semantics=("arbitrary",), vmem_limit_bytes=VMEM_LIMIT),
        name="in_proj",
    )(h, gain, w, *att_tables, *ret_tables)


ATT_PAD = HALF_WIDTH * max(DILATIONS)
PREP_ROWS = 512
ATT_Q_SCALE = ATT_HEAD_DIM ** -0.5 * 1.4426950408889634
ATT_UNROLL = 8


def _attn_kernel(q_ref, k_ref, v_ref, o_ref, qs, kp, vp, acc, mrun, lrun, band):
    s_len = q_ref.shape[0]
    zeros = jnp.zeros((ATT_PAD, LANES), F32)
    kp[0:ATT_PAD, :] = zeros
    kp[ATT_PAD + s_len:, :] = zeros
    vp[0:ATT_PAD, :] = zeros
    vp[ATT_PAD + s_len:, :] = zeros

    d = (lax.broadcasted_iota(jnp.int32, (QBLK, KBLK), 1)
         - lax.broadcasted_iota(jnp.int32, (QBLK, KBLK), 0))
    for n, lo in enumerate((0, -HALF_WIDTH, QBLK - HALF_WIDTH)):
        band[n] = jnp.where((d >= lo) & (d <= lo + 2 * HALF_WIDTH), 0.0, MASK_VALUE).astype(F32)

    def prep(j, carry):
        r0 = pl.multiple_of(j * PREP_ROWS, PREP_ROWS)
        rows = pl.ds(r0, PREP_ROWS)
        qs[rows, :] = q_ref[rows, :].astype(F32)
        kp[pl.ds(ATT_PAD + r0, PREP_ROWS), :] = k_ref[rows, :].astype(F32)
        vp[pl.ds(ATT_PAD + r0, PREP_ROWS), :] = v_ref[rows, :].astype(F32)
        return carry

    lax.fori_loop(0, s_len // PREP_ROWS, prep, 0)

    lane_q = lax.broadcasted_iota(jnp.int32, (QBLK, LANES), 1)
    head_q = (lane_q < ATT_HEAD_DIM, lane_q >= ATT_HEAD_DIM)
    key_j = lax.broadcasted_iota(jnp.int32, (1, KBLK), 1)

    def rows_of(start, size, r):
        return pl.ds(start, size) if r == 1 else pl.ds(start, size, stride=r)

    def block(q_rows, kb, v2, bias, first):
        qb = qs[q_rows, :]
        q2 = jnp.concatenate([jnp.where(head_q[0], qb, 0.0), jnp.where(head_q[1], qb, 0.0)],
                             axis=0).astype(BF16)
        s = lax.dot_general(q2, kb, (((1,), (1,)), ((), ())), preferred_element_type=F32)
        s = s + jnp.concatenate([bias, bias], axis=0)
        m = jnp.max(s, axis=-1, keepdims=True)
        p = jnp.exp2(s - m).astype(BF16)
        nd = jnp.dot(p, v2, preferred_element_type=F32)
        num = jnp.where(head_q[0], nd[:QBLK, :LANES], nd[QBLK:, :LANES])
        den = jnp.where(head_q[0], nd[:QBLK, LANES:], nd[QBLK:, LANES:])
        mx = jnp.where(head_q[0], m[:QBLK], m[QBLK:])
        if first:
            acc[q_rows, :] = num
            lrun[q_rows, :] = den
            mrun[q_rows, :] = mx
        else:
            m_old = mrun[q_rows, :]
            m_new = jnp.maximum(m_old, mx)
            a_old = jnp.exp2(m_old - m_new)
            a_new = jnp.exp2(mx - m_new)
            acc[q_rows, :] = a_old * acc[q_rows, :] + a_new * num
            lrun[q_rows, :] = a_old * lrun[q_rows, :] + a_new * den
            mrun[q_rows, :] = m_new

    def load_keys(k_rows):
        vb = vp[k_rows, :].astype(BF16)
        return kp[k_rows, :].astype(BF16), jnp.concatenate([vb, jnp.ones_like(vb)], axis=1)

    def run_group(r, first):
        sub_len = s_len // r
        log2r = r.bit_length() - 1

        def body(t, carry):
            c = jnp.bitwise_and(t, r - 1)
            i = lax.shift_right_logical(t, log2r)
            q_start = c + (r * QBLK) * i
            kb, v2 = load_keys(rows_of(ATT_PAD - r * HALF_WIDTH + q_start, KBLK, r))
            key_pos = QBLK * i - HALF_WIDTH + key_j
            bias = band[0] + jnp.where((key_pos >= 0) & (key_pos < sub_len), 0.0, MASK_VALUE)
            block(rows_of(q_start, QBLK, r), kb, v2, bias, first)
            return carry

        def whole(c, carry):
            kb, v2 = load_keys(rows_of(ATT_PAD + c, KBLK, r))
            for i in range(KBLK // QBLK):
                block(rows_of(c + (r * QBLK) * i, QBLK, r), kb, v2, band[1 + i], first)
            return carry

        if sub_len == KBLK:
            lax.fori_loop(0, r, whole, 0, unroll=ATT_UNROLL // 2)
        else:
            lax.fori_loop(0, s_len // QBLK, body, 0, unroll=ATT_UNROLL)

    for g, r in enumerate(sorted(DILATIONS, reverse=True)):
        run_group(r, g == 0)

    def finish(j, carry):
        rows = pl.ds(pl.multiple_of(j * PREP_ROWS, PREP_ROWS), PREP_ROWS)
        o_ref[rows, :] = (acc[rows, :] / lrun[rows, :]).astype(o_ref.dtype)
        return carry

    lax.fori_loop(0, s_len // PREP_ROWS, finish, 0)


def _attention(qkv):
    b, s, _ = qkv.shape
    pairs = ATT_W // LANES

    def col(section):
        return pl.BlockSpec((None, s, LANES), lambda bi, hp: (bi, 0, section * pairs + hp))

    return pl.pallas_call(
        _attn_kernel,
        grid=(b, pairs),
        in_specs=[col(0), col(1), col(2)],
        out_specs=pl.BlockSpec((None, s, LANES), lambda bi, hp: (bi, 0, hp)),
        out_shape=jax.ShapeDtypeStruct((b, s, ATT_W), BF16),
        scratch_shapes=[
            pltpu.VMEM((s, LANES), F32),
            pltpu.VMEM((s + 2 * ATT_PAD, LANES), F32),
            pltpu.VMEM((s + 2 * ATT_PAD, LANES), F32),
            pltpu.VMEM((s, LANES), F32),
            pltpu.VMEM((s, LANES), F32),
            pltpu.VMEM((s, LANES), F32),
            pltpu.VMEM((3, QBLK, KBLK), F32),
        ],
        compiler_params=pltpu.CompilerParams(
            dimension_semantics=("arbitrary", "arbitrary"), vmem_limit_bytes=VMEM_LIMIT),
        name="attention",
    )(qkv, qkv, qkv)


RET_UNROLL = 2


def _ret_kernel(q_ref, k_ref, v_ref, g_ref, dl_ref, o_ref, kt, states, sf, sb):
    s_len = q_ref.shape[0]
    c = RET_CHUNK
    n_chunks = s_len // c

    def prep(t, carry):
        rows = pl.ds(pl.multiple_of(t * c, c), c)
        kt[t] = k_ref[rows, :].astype(F32).T
        return carry

    lax.fori_loop(0, n_chunks, prep, 0, unroll=4)

    x = dl_ref[...]
    log_gamma = jnp.minimum(x, 0.0) - jnp.log(1.0 + jnp.exp(-jnp.abs(x)))
    lgf, lgb = log_gamma[0:1, :], log_gamma[1:2, :]
    row_i = lax.broadcasted_iota(jnp.int32, (c, LANES), 0)
    lane_i = lax.broadcasted_iota(jnp.int32, (c, LANES), 1)
    row, lane = row_i.astype(F32), lane_i.astype(F32)
    head_lane = (lane_i < RET_QK_DIM, lane_i >= RET_QK_DIM)

    def per_row(lg):
        return jnp.where(row_i < RET_QK_DIM, lg[:, :LANES], lg[:, LANES:])

    def per_lane(lg):
        return jnp.where(head_lane[0], lg[:, :LANES], lg[:, LANES:])

    zeta_f = jnp.exp((c - 1.0 - lane) * per_row(lgf))
    zeta_b = jnp.exp(lane * per_row(lgb))
    xi_f = jnp.exp((row + 1.0) * per_lane(lgf))
    xi_b = jnp.exp((c - row) * per_lane(lgb))
    diff = row - lane
    decay = jnp.concatenate(
        [jnp.exp(jnp.where(diff >= 0, diff * lgf[:, h * LANES:(h + 1) * LANES],
                           -diff * lgb[:, h * LANES:(h + 1) * LANES])) for h in range(2)], axis=0)

    srow = lax.broadcasted_iota(jnp.int32, (LANES, 2 * LANES), 0)
    scol = lax.broadcasted_iota(jnp.int32, (LANES, 2 * LANES), 1)
    same_head = (srow < RET_QK_DIM) == (scol < LANES)
    gc_f = jnp.where(same_head, jnp.exp(c * lgf), 0.0)
    gc_b = jnp.where(same_head, jnp.exp(c * lgb), 0.0)

    sf[...] = jnp.zeros_like(sf)
    sb[...] = jnp.zeros_like(sb)

    def scan(t, carry):
        for state, zeta, gc, tt, half_rows in ((sf, zeta_f, gc_f, t, slice(0, LANES)),
                                               (sb, zeta_b, gc_b, n_chunks - 1 - t, slice(LANES, 2 * LANES))):
            rows = pl.ds(pl.multiple_of(tt * c, c), c)
            old = state[...]
            states[tt, half_rows, :] = old.astype(BF16)
            kv = jnp.dot((kt[tt] * zeta).astype(BF16), v_ref[rows, :], preferred_element_type=F32)
            state[...] = gc * old + jnp.where(same_head, kv, 0.0)
        return carry

    lax.fori_loop(0, n_chunks, scan, 0, unroll=2 * RET_UNROLL)

    vrow = lax.broadcasted_iota(jnp.int32, (2 * c, 2 * LANES), 0)
    vcol = lax.broadcasted_iota(jnp.int32, (2 * c, 2 * LANES), 1)
    v_diag = (vrow < c) == (vcol < LANES)

    def combine(t, carry):
        rows = pl.ds(pl.multiple_of(t * c, c), c)
        q = q_ref[rows, :].astype(F32)
        q2 = jnp.concatenate([jnp.where(head_lane[0], q, 0.0), jnp.where(head_lane[1], q, 0.0)],
                             axis=0).astype(BF16)
        s = jnp.dot(q2, kt[t].astype(BF16), preferred_element_type=F32) * decay
        s_wide = jnp.concatenate([s[:c], s[c:]], axis=1).astype(BF16)
        v = v_ref[rows, :]
        v_blocks = jnp.where(v_diag, jnp.concatenate([v, v], axis=0), 0.0)
        y = jnp.dot(s_wide, v_blocks, preferred_element_type=F32)
        q_cross = jnp.concatenate([q * xi_f, q * xi_b], axis=1).astype(BF16)
        y = y + jnp.dot(q_cross, states[t], preferred_element_type=F32)
        for hh in range(2):
            cols = slice(hh * LANES, (hh + 1) * LANES)
            yh = y[:, cols]
            mu = jnp.mean(yh, axis=-1, keepdims=True)
            yc = yh - mu
            var = jnp.mean(yc * yc, axis=-1, keepdims=True)
            yn = yc * lax.rsqrt(var + GN_EPS)
            o_ref[rows, cols] = (g_ref[rows, cols].astype(F32) * yn).astype(o_ref.dtype)
        return carry

    lax.fori_loop(0, n_chunks, combine, 0, unroll=2 * RET_UNROLL)


def _retention(qk, v, g, decay_lanes):
    b, s, _ = qk.shape
    pairs = RET_QK_W // LANES
    n_chunks = s // RET_CHUNK
    wide = pl.BlockSpec((None, s, 2 * LANES), lambda bi, hp: (bi, 0, hp))
    return pl.pallas_call(
        _ret_kernel,
        grid=(b, pairs),
        in_specs=[
            pl.BlockSpec((None, s, LANES), lambda bi, hp: (bi, 0, hp)),
            pl.BlockSpec((None, s, LANES), lambda bi, hp: (bi, 0, pairs + hp)),
            wide, wide,
            pl.BlockSpec((2, 2 * LANES), lambda bi, hp: (0, hp)),
        ],
        out_specs=wide,
        out_shape=jax.ShapeDtypeStruct((b, s, RET_V_W), BF16),
        scratch_shapes=[
            pltpu.VMEM((n_chunks, LANES, RET_CHUNK), F32),
            pltpu.VMEM((n_chunks, 2 * LANES, 2 * LANES), BF16),
            pltpu.VMEM((LANES, 2 * LANES), F32),
            pltpu.VMEM((LANES, 2 * LANES), F32),
        ],
        compiler_params=pltpu.CompilerParams(
            dimension_semantics=("arbitrary", "arbitrary"), vmem_limit_bytes=VMEM_LIMIT),
        name="retention",
    )(qk, qk, v, g, decay_lanes)


POST_TM = 512
FF_CHUNK = 1024


def _post_kernel(h_ref, att_ref, ret_ref, ga_ref, gb_ref, p_ref,
                 w_att_ref, w_ret_ref, w_out_ref, w_up_ref, w_down_ref, w_gate_ref, w_proj_ref,
                 n_post_ref, n_mlp_pre_ref, n_mlp_post_ref, n_ple_ref, o_ref, u_ref):
    a = jnp.dot(att_ref[...], w_att_ref[...], preferred_element_type=F32)
    r = jnp.dot(ret_ref[...], w_ret_ref[...], preferred_element_type=F32)
    merged = (_sigmoid(ga_ref[...].astype(F32)) * a + _sigmoid(gb_ref[...].astype(F32)) * r)
    y = jnp.dot(merged.astype(BF16), w_out_ref[...], preferred_element_type=F32)
    h = h_ref[...] + _rms(y) * n_post_ref[...]

    u_ref[...] = (_rms(h) * n_mlp_pre_ref[...]).astype(BF16)
    ff = jnp.zeros((POST_TM, D_MODEL), F32)
    for c in range(0, D_FF, FF_CHUNK):
        z = jnp.dot(u_ref[...], w_up_ref[:, c:c + FF_CHUNK], preferred_element_type=F32)
        z = jnp.maximum(z, 0.0)
        ff = ff + jnp.dot((z * z).astype(BF16), w_down_ref[c:c + FF_CHUNK, :],
                          preferred_element_type=F32)
    h = h + _rms(ff) * n_mlp_post_ref[...]

    gate = _sigmoid(jnp.dot(_rms(h).astype(BF16), w_gate_ref[...], preferred_element_type=F32))
    pe = jnp.dot(p_ref[...].astype(BF16), w_proj_ref[...], preferred_element_type=F32)
    o_ref[...] = h + gate * (_rms(pe) * n_ple_ref[...])


def _post(layer, h, att, ret, ga, gb, p, w_att, w_ret, w_out, w_up, w_down, w_gate, w_proj,
          n_post, n_mlp_pre, n_mlp_post, n_ple):
    t = h.shape[0]

    def rows(width):
        return pl.BlockSpec((POST_TM, width), lambda i: (i, 0))

    def whole(arr):
        return pl.BlockSpec((None,) + arr.shape[1:], lambda i: (layer, 0, 0), pipeline_mode=pl.Buffered(1))

    weights = (w_att, w_ret, w_out, w_up, w_down, w_gate, w_proj)
    gains = (n_post, n_mlp_pre, n_mlp_post, n_ple)
    return pl.pallas_call(
        _post_kernel,
        grid=(t // POST_TM,),
        in_specs=[rows(D_MODEL), rows(ATT_W), rows(RET_V_W), rows(D_MODEL), rows(D_MODEL),
                  pl.BlockSpec((None, POST_TM, PLE_DIM), lambda i: (layer, i, 0))]
        + [whole(w) for w in weights] + [whole(g) for g in gains],
        out_specs=rows(D_MODEL),
        out_shape=jax.ShapeDtypeStruct((t, D_MODEL), F32),
        scratch_shapes=[pltpu.VMEM((POST_TM, D_MODEL), BF16)],
        compiler_params=pltpu.CompilerParams(
            dimension_semantics=("arbitrary",), vmem_limit_bytes=VMEM_LIMIT),
        name="post",
    )(h, att, ret, ga, gb, p, *weights, *gains)


def _rotary_tables(s, rot_dim, head_dim, theta):
    half = rot_dim // 2
    inv = theta ** (-jnp.arange(half, dtype=F32) * 2.0 / rot_dim)
    ang = jnp.arange(s).astype(F32)[:, None] * inv[None, :]
    within = jnp.arange(LANES) % head_dim
    cos = jnp.cos(ang)[:, within % half]
    sin = jnp.sin(ang)[:, within % half]
    rotated = within < rot_dim
    lower = within < half
    cos_t = jnp.where(rotated, cos, 1.0)
    sin_up = jnp.where(lower, -sin, 0.0)
    sin_dn = jnp.where(rotated & ~lower, sin, 0.0)
    return cos_t, sin_up, sin_dn


def kernel(x, p, w_in, w_att_out, w_ret_out, w_out, w_mlp_up, w_mlp_down, w_ple_gate, w_ple_proj,
           ret_decay_logit, norm_mix_pre, norm_mix_post, norm_mlp_pre, norm_mlp_post, norm_ple):
    b, s, d = x.shape
    depth = w_in.shape[0]
    t = b * s
    att_tables = _rotary_tables(s, ATT_ROPE_DIM, ATT_HEAD_DIM, ATT_ROPE_THETA)
    ret_tables = _rotary_tables(s, RET_QK_DIM, RET_QK_DIM, RET_THETA)

    def stacked(w):
        return w.astype(BF16)

    def gains(g):
        return g.reshape(depth, 1, d)

    w_in_b = stacked(w_in)
    post_weights = tuple(stacked(w) for w in (w_att_out, w_ret_out, w_out, w_mlp_up, w_mlp_down,
                                              w_ple_gate, w_ple_proj))
    post_gains = tuple(gains(g) for g in (norm_mix_post, norm_mlp_pre, norm_mlp_post, norm_ple))
    p_rows = p.reshape(depth, t, -1)

    h = x.reshape(t, d)
    for i in range(depth):
        qkv, rqk, rv, rg, ga, gb = _in_proj(i, h, gains(norm_mix_pre), w_in_b, att_tables, ret_tables, s)
        att = _attention(qkv.reshape(b, s, -1))
        decay_lanes = jnp.repeat(ret_decay_logit[i], RET_V_DIM, axis=-1)
        ret = _retention(rqk.reshape(b, s, -1), rv.reshape(b, s, -1), rg.reshape(b, s, -1),
                         decay_lanes)
        h = _post(i, h, att.reshape(t, -1), ret.reshape(t, -1), ga, gb, p_rows,
                  *post_weights, *post_gains)
    return h.reshape(b, s, d)
```

```python
import functools

import jax
import jax.numpy as jnp
from jax import lax
from jax.experimental import pallas as pl
from jax.experimental.pallas import tpu as pltpu

D_MODEL = 1024
ATT_HEADS = 8
ATT_HEAD_DIM = 64
ATT_ROPE_DIM = 16
ATT_ROPE_THETA = 500000.0
DILATIONS = (1, 4, 16)
HALF_WIDTH = 64
RET_HEADS = 8
RET_QK_DIM = 64
RET_V_DIM = 128
RET_THETA = 10000.0
RET_CHUNK = 128
D_FF = 4096
PLE_DIM = 256
NORM_EPS = 1e-6
GN_EPS = 1e-5
MASK_VALUE = -1e30

ATT_W = ATT_HEADS * ATT_HEAD_DIM
RET_QK_W = RET_HEADS * RET_QK_DIM
RET_V_W = RET_HEADS * RET_V_DIM
IN_WIDTH = 3 * ATT_W + 2 * RET_QK_W + 2 * RET_V_W + 2 * D_MODEL

LANES = 128
QBLK = 128
KBLK = QBLK + 2 * HALF_WIDTH
VMEM_LIMIT = 56 * 1024 * 1024

F32 = jnp.float32
BF16 = jnp.bfloat16


def _rms(x):
    return x * lax.rsqrt(jnp.mean(x * x, axis=-1, keepdims=True) + NORM_EPS)


def _sigmoid(x):
    return 1.0 / (1.0 + jnp.exp(-x))


def _rotate(x, cos, sin_up, sin_dn, shift):
    return (x * cos + pltpu.roll(x, LANES - shift, 1) * sin_up
            + pltpu.roll(x, shift, 1) * sin_dn)


IN_TM = 512
IN_TN = 512
_IN_OUT_WIDTHS = (3 * ATT_W, 2 * RET_QK_W, RET_V_W, RET_V_W, D_MODEL, D_MODEL)


def _in_proj_kernel(x_ref, g_ref, w_ref, ac_ref, au_ref, ad_ref, rc_ref, ru_ref, rd_ref, *rest):
    out_refs, u_ref = rest[:-1], rest[-1]
    u_ref[...] = (_rms(x_ref[...]) * g_ref[...]).astype(BF16)

    def rotary(tables, half):
        def apply(y):
            cos, sup, sdn = (tb[...] for tb in tables)
            return jnp.concatenate(
                [_rotate(y[:, j:j + LANES], cos, sup, sdn, half) for j in range(0, IN_TN, LANES)], axis=1)
        return apply

    att_rot = rotary((ac_ref, au_ref, ad_ref), ATT_ROPE_DIM // 2)
    ret_rot = rotary((rc_ref, ru_ref, rd_ref), RET_QK_DIM // 2)
    epilogues = {(0, 0): lambda y: att_rot(y) * ATT_Q_SCALE, (0, ATT_W): att_rot,
                 (1, 0): ret_rot, (1, RET_QK_W): lambda y: ret_rot(y) * (RET_QK_DIM ** -0.5),
                 (3, 0): lambda y: y * _sigmoid(y), (3, IN_TN): lambda y: y * _sigmoid(y)}
    off = 0
    for n, (o_ref, width) in enumerate(zip(out_refs, _IN_OUT_WIDTHS)):
        for c in range(0, width, IN_TN):
            y = jnp.dot(u_ref[...], w_ref[:, off + c:off + c + IN_TN], preferred_element_type=F32)
            o_ref[:, c:c + IN_TN] = epilogues.get((n, c), lambda y: y)(y).astype(BF16)
        off += width


def _in_proj(layer, h, gain, w, att_tables, ret_tables, s_len):
    t = h.shape[0]
    table = pl.BlockSpec((IN_TM, LANES), lambda i: (i % (s_len // IN_TM), 0))
    return pl.pallas_call(
        _in_proj_kernel,
        grid=(t // IN_TM,),
        in_specs=[
            pl.BlockSpec((IN_TM, D_MODEL), lambda i: (i, 0)),
            pl.BlockSpec((None, 1, D_MODEL), lambda i: (layer, 0, 0)),
            pl.BlockSpec((None, D_MODEL, IN_WIDTH), lambda i: (layer, 0, 0), pipeline_mode=pl.Buffered(1)),
        ] + [table] * 6,
        out_specs=[pl.BlockSpec((IN_TM, wd), lambda i: (i, 0)) for wd in _IN_OUT_WIDTHS],
        out_shape=[jax.ShapeDtypeStruct((t, wd), BF16) for wd in _IN_OUT_WIDTHS],
        scratch_shapes=[pltpu.VMEM((IN_TM, D_MODEL), BF16)],
        compiler_params=pltpu.CompilerParams(
            dimension_semantics=("arbitrary",), vmem_limit_bytes=VMEM_LIMIT),
        name="in_proj",
    )(h, gain, w, *att_tables, *ret_tables)


ATT_PAD = HALF_WIDTH * max(DILATIONS)
PREP_ROWS = 512
ATT_Q_SCALE = ATT_HEAD_DIM ** -0.5 * 1.4426950408889634
ATT_UNROLL = 8


def _attn_kernel(q_ref, k_ref, v_ref, o_ref, qs, kp, vp, acc, mrun, lrun, band):
    s_len = q_ref.shape[0]
    zeros = jnp.zeros((ATT_PAD, LANES), F32)
    kp[0:ATT_PAD, :] = zeros
    kp[ATT_PAD + s_len:, :] = zeros
    vp[0:ATT_PAD, :] = zeros
    vp[ATT_PAD + s_len:, :] = zeros

    d = (lax.broadcasted_iota(jnp.int32, (QBLK, KBLK), 1)
         - lax.broadcasted_iota(jnp.int32, (QBLK, KBLK), 0))
    for n, lo in enumerate((0, -HALF_WIDTH, QBLK - HALF_WIDTH)):
        band[n] = jnp.where((d >= lo) & (d <= lo + 2 * HALF_WIDTH), 0.0, MASK_VALUE).astype(F32)

    def prep(j, carry):
        r0 = pl.multiple_of(j * PREP_ROWS, PREP_ROWS)
        rows = pl.ds(r0, PREP_ROWS)
        qs[rows, :] = q_ref[rows, :].astype(F32)
        kp[pl.ds(ATT_PAD + r0, PREP_ROWS), :] = k_ref[rows, :].astype(F32)
        vp[pl.ds(ATT_PAD + r0, PREP_ROWS), :] = v_ref[rows, :].astype(F32)
        return carry

    lax.fori_loop(0, s_len // PREP_ROWS, prep, 0)

    lane_q = lax.broadcasted_iota(jnp.int32, (QBLK, LANES), 1)
    head_q = (lane_q < ATT_HEAD_DIM, lane_q >= ATT_HEAD_DIM)
    key_j = lax.broadcasted_iota(jnp.int32, (1, KBLK), 1)

    def rows_of(start, size, r):
        return pl.ds(start, size) if r == 1 else pl.ds(start, size, stride=r)

    def block(q_rows, kb, v2, bias, first, out_rows=None):
        qb = qs[q_rows, :]
        q2 = jnp.concatenate([jnp.where(head_q[0], qb, 0.0), jnp.where(head_q[1], qb, 0.0)],
                             axis=0).astype(BF16)
        s = lax.dot_general(q2, kb, (((1,), (1,)), ((), ())), preferred_element_type=F32)
        s = s + jnp.concatenate([bias, bias], axis=0)
        m = jnp.max(s, axis=-1, keepdims=True)
        p = jnp.exp2(s - m).astype(BF16)
        nd = jnp.dot(p, v2, preferred_element_type=F32)
        num = jnp.where(head_q[0], nd[:QBLK, :LANES], nd[QBLK:, :LANES])
        den = jnp.where(head_q[0], nd[:QBLK, LANES:], nd[QBLK:, LANES:])
        mx = jnp.where(head_q[0], m[:QBLK], m[QBLK:])
        if first:
            acc[q_rows, :] = num
            lrun[q_rows, :] = den
            mrun[q_rows, :] = mx
        else:
            m_old = mrun[q_rows, :]
            m_new = jnp.maximum(m_old, mx)
            a_old = jnp.exp2(m_old - m_new)
            a_new = jnp.exp2(mx - m_new)
            num = a_old * acc[q_rows, :] + a_new * num
            den = a_old * lrun[q_rows, :] + a_new * den
            if out_rows is not None:
                o_ref[out_rows, :] = (num / den).astype(o_ref.dtype)
            else:
                acc[q_rows, :] = num
                lrun[q_rows, :] = den
                mrun[q_rows, :] = m_new

    def load_keys(k_rows):
        vb = vp[k_rows, :].astype(BF16)
        return kp[k_rows, :].astype(BF16), jnp.concatenate([vb, jnp.ones_like(vb)], axis=1)

    def run_group(r, first, last):
        direct = last and not first and r == 1
        sub_len = s_len // r
        log2r = r.bit_length() - 1

        def body(t, carry):
            c = jnp.bitwise_and(t, r - 1)
            i = lax.shift_right_logical(t, log2r)
            q_start = c + (r * QBLK) * i
            kb, v2 = load_keys(rows_of(ATT_PAD - r * HALF_WIDTH + q_start, KBLK, r))
            key_pos = QBLK * i - HALF_WIDTH + key_j
            bias = band[0] + jnp.where((key_pos >= 0) & (key_pos < sub_len), 0.0, MASK_VALUE)
            out_rows = pl.ds(pl.multiple_of(q_start, QBLK), QBLK) if direct else None
            block(rows_of(q_start, QBLK, r), kb, v2, bias, first, out_rows)
            return carry

        def whole(c, carry):
            kb, v2 = load_keys(rows_of(ATT_PAD + c, KBLK, r))
            for i in range(KBLK // QBLK):
                block(rows_of(c + (r * QBLK) * i, QBLK, r), kb, v2, band[1 + i], first)
            return carry

        if sub_len == KBLK:
            lax.fori_loop(0, r, whole, 0, unroll=ATT_UNROLL // 2)
        else:
            lax.fori_loop(0, s_len // QBLK, body, 0, unroll=ATT_UNROLL)

    order = sorted(DILATIONS, reverse=True)
    for g, r in enumerate(order):
        run_group(r, g == 0, g == len(order) - 1)
    if len(order) > 1 and order[-1] == 1:
        return

    def finish(j, carry):
        rows = pl.ds(pl.multiple_of(j * PREP_ROWS, PREP_ROWS), PREP_ROWS)
        o_ref[rows, :] = (acc[rows, :] / lrun[rows, :]).astype(o_ref.dtype)
        return carry

    lax.fori_loop(0, s_len // PREP_ROWS, finish, 0)


def _attention(qkv):
    b, s, _ = qkv.shape
    pairs = ATT_W // LANES

    def col(section):
        return pl.BlockSpec((None, s, LANES), lambda bi, hp: (bi, 0, section * pairs + hp))

    return pl.pallas_call(
        _attn_kernel,
        grid=(b, pairs),
        in_specs=[col(0), col(1), col(2)],
        out_specs=pl.BlockSpec((None, s, LANES), lambda bi, hp: (bi, 0, hp)),
        out_shape=jax.ShapeDtypeStruct((b, s, ATT_W), BF16),
        scratch_shapes=[
            pltpu.VMEM((s, LANES), F32),
            pltpu.VMEM((s + 2 * ATT_PAD, LANES), F32),
            pltpu.VMEM((s + 2 * ATT_PAD, LANES), F32),
            pltpu.VMEM((s, LANES), F32),
            pltpu.VMEM((s, LANES), F32),
            pltpu.VMEM((s, LANES), F32),
            pltpu.VMEM((3, QBLK, KBLK), F32),
        ],
        compiler_params=pltpu.CompilerParams(
            dimension_semantics=("arbitrary", "arbitrary"), vmem_limit_bytes=VMEM_LIMIT),
        name="attention",
    )(qkv, qkv, qkv)


RET_UNROLL = 2


def _ret_kernel(q_ref, k_ref, v_ref, g_ref, dl_ref, o_ref, kt, states, sf, sb):
    s_len = q_ref.shape[0]
    c = RET_CHUNK
    n_chunks = s_len // c

    def prep(t, carry):
        rows = pl.ds(pl.multiple_of(t * c, c), c)
        kt[t] = k_ref[rows, :].astype(F32).T
        return carry

    lax.fori_loop(0, n_chunks, prep, 0, unroll=4)

    x = dl_ref[...]
    log_gamma = jnp.minimum(x, 0.0) - jnp.log(1.0 + jnp.exp(-jnp.abs(x)))
    lgf, lgb = log_gamma[0:1, :], log_gamma[1:2, :]
    row_i = lax.broadcasted_iota(jnp.int32, (c, LANES), 0)
    lane_i = lax.broadcasted_iota(jnp.int32, (c, LANES), 1)
    row, lane = row_i.astype(F32), lane_i.astype(F32)
    head_lane = (lane_i < RET_QK_DIM, lane_i >= RET_QK_DIM)

    def per_row(lg):
        return jnp.where(row_i < RET_QK_DIM, lg[:, :LANES], lg[:, LANES:])

    def per_lane(lg):
        return jnp.where(head_lane[0], lg[:, :LANES], lg[:, LANES:])

    zeta_f = jnp.exp((c - 1.0 - lane) * per_row(lgf))
    zeta_b = jnp.exp(lane * per_row(lgb))
    xi_f = jnp.exp((row + 1.0) * per_lane(lgf))
    xi_b = jnp.exp((c - row) * per_lane(lgb))
    diff = row - lane
    decay = jnp.concatenate(
        [jnp.exp(jnp.where(diff >= 0, diff * lgf[:, h * LANES:(h + 1) * LANES],
                           -diff * lgb[:, h * LANES:(h + 1) * LANES])) for h in range(2)], axis=0)

    srow = lax.broadcasted_iota(jnp.int32, (LANES, 2 * LANES), 0)
    scol = lax.broadcasted_iota(jnp.int32, (LANES, 2 * LANES), 1)
    same_head = (srow < RET_QK_DIM) == (scol < LANES)
    gc_f = jnp.where(same_head, jnp.exp(c * lgf), 0.0)
    gc_b = jnp.where(same_head, jnp.exp(c * lgb), 0.0)

    sf[...] = jnp.zeros_like(sf)
    sb[...] = jnp.zeros_like(sb)

    def scan(t, carry):
        for state, zeta, gc, tt, half_rows in ((sf, zeta_f, gc_f, t, slice(0, LANES)),
                                               (sb, zeta_b, gc_b, n_chunks - 1 - t, slice(LANES, 2 * LANES))):
            rows = pl.ds(pl.multiple_of(tt * c, c), c)
            old = state[...]
            states[tt, half_rows, :] = old.astype(BF16)
            kv = jnp.dot((kt[tt] * zeta).astype(BF16), v_ref[rows, :], preferred_element_type=F32)
            state[...] = gc * old + jnp.where(same_head, kv, 0.0)
        return carry

    lax.fori_loop(0, n_chunks, scan, 0, unroll=2 * RET_UNROLL)

    vrow = lax.broadcasted_iota(jnp.int32, (2 * c, 2 * LANES), 0)
    vcol = lax.broadcasted_iota(jnp.int32, (2 * c, 2 * LANES), 1)
    v_diag = (vrow < c) == (vcol < LANES)

    def combine(t, carry):
        rows = pl.ds(pl.multiple_of(t * c, c), c)
        q = q_ref[rows, :].astype(F32)
        q2 = jnp.concatenate([jnp.where(head_lane[0], q, 0.0), jnp.where(head_lane[1], q, 0.0)],
                             axis=0).astype(BF16)
        s = jnp.dot(q2, kt[t].astype(BF16), preferred_element_type=F32) * decay
        s_wide = jnp.concatenate([s[:c], s[c:]], axis=1).astype(BF16)
        v = v_ref[rows, :]
        v_blocks = jnp.where(v_diag, jnp.concatenate([v, v], axis=0), 0.0)
        y = jnp.dot(s_wide, v_blocks, preferred_element_type=F32)
        q_cross = jnp.concatenate([q * xi_f, q * xi_b], axis=1).astype(BF16)
        y = y + jnp.dot(q_cross, states[t], preferred_element_type=F32)
        for hh in range(2):
            cols = slice(hh * LANES, (hh + 1) * LANES)
            yh = y[:, cols]
            mu = jnp.mean(yh, axis=-1, keepdims=True)
            yc = yh - mu
            var = jnp.mean(yc * yc, axis=-1, keepdims=True)
            yn = yc * lax.rsqrt(var + GN_EPS)
            o_ref[rows, cols] = (g_ref[rows, cols].astype(F32) * yn).astype(o_ref.dtype)
        return carry

    lax.fori_loop(0, n_chunks, combine, 0, unroll=2 * RET_UNROLL)


def _retention(qk, v, g, decay_lanes):
    b, s, _ = qk.shape
    pairs = RET_QK_W // LANES
    n_chunks = s // RET_CHUNK
    wide = pl.BlockSpec((None, s, 2 * LANES), lambda bi, hp: (bi, 0, hp))
    return pl.pallas_call(
        _ret_kernel,
        grid=(b, pairs),
        in_specs=[
            pl.BlockSpec((None, s, LANES), lambda bi, hp: (bi, 0, hp)),
            pl.BlockSpec((None, s, LANES), lambda bi, hp: (bi, 0, pairs + hp)),
            wide, wide,
            pl.BlockSpec((2, 2 * LANES), lambda bi, hp: (0, hp)),
        ],
        out_specs=wide,
        out_shape=jax.ShapeDtypeStruct((b, s, RET_V_W), BF16),
        scratch_shapes=[
            pltpu.VMEM((n_chunks, LANES, RET_CHUNK), F32),
            pltpu.VMEM((n_chunks, 2 * LANES, 2 * LANES), BF16),
            pltpu.VMEM((LANES, 2 * LANES), F32),
            pltpu.VMEM((LANES, 2 * LANES), F32),
        ],
        compiler_params=pltpu.CompilerParams(
            dimension_semantics=("arbitrary", "arbitrary"), vmem_limit_bytes=VMEM_LIMIT),
        name="retention",
    )(qk, qk, v, g, decay_lanes)


POST_TM = 512
FF_CHUNK = 1024


def _post_kernel(h_ref, att_ref, ret_ref, ga_ref, gb_ref, p_ref,
                 w_att_ref, w_ret_ref, w_out_ref, w_up_ref, w_down_ref, w_gate_ref, w_proj_ref,
                 n_post_ref, n_mlp_pre_ref, n_mlp_post_ref, n_ple_ref, o_ref, u_ref):
    a = jnp.dot(att_ref[...], w_att_ref[...], preferred_element_type=F32)
    r = jnp.dot(ret_ref[...], w_ret_ref[...], preferred_element_type=F32)
    merged = (_sigmoid(ga_ref[...].astype(F32)) * a + _sigmoid(gb_ref[...].astype(F32)) * r)
    y = jnp.dot(merged.astype(BF16), w_out_ref[...], preferred_element_type=F32)
    h = h_ref[...] + _rms(y) * n_post_ref[...]

    u_ref[...] = (_rms(h) * n_mlp_pre_ref[...]).astype(BF16)
    ff = jnp.zeros((POST_TM, D_MODEL), F32)
    for c in range(0, D_FF, FF_CHUNK):
        z = jnp.dot(u_ref[...], w_up_ref[:, c:c + FF_CHUNK], preferred_element_type=F32)
        z = jnp.maximum(z, 0.0)
        ff = ff + jnp.dot((z * z).astype(BF16), w_down_ref[c:c + FF_CHUNK, :],
                          preferred_element_type=F32)
    h = h + _rms(ff) * n_mlp_post_ref[...]

    gate = _sigmoid(jnp.dot(_rms(h).astype(BF16), w_gate_ref[...], preferred_element_type=F32))
    pe = jnp.dot(p_ref[...].astype(BF16), w_proj_ref[...], preferred_element_type=F32)
    o_ref[...] = h + gate * (_rms(pe) * n_ple_ref[...])


def _post(layer, h, att, ret, ga, gb, p, w_att, w_ret, w_out, w_up, w_down, w_gate, w_proj,
          n_post, n_mlp_pre, n_mlp_post, n_ple):
    t = h.shape[0]

    def rows(width):
        return pl.BlockSpec((POST_TM, width), lambda i: (i, 0))

    def whole(arr):
        return pl.BlockSpec((None,) + arr.shape[1:], lambda i: (layer, 0, 0), pipeline_mode=pl.Buffered(1))

    weights = (w_att, w_ret, w_out, w_up, w_down, w_gate, w_proj)
    gains = (n_post, n_mlp_pre, n_mlp_post, n_ple)
    return pl.pallas_call(
        _post_kernel,
        grid=(t // POST_TM,),
        in_specs=[rows(D_MODEL), rows(ATT_W), rows(RET_V_W), rows(D_MODEL), rows(D_MODEL),
                  pl.BlockSpec((None, POST_TM, PLE_DIM), lambda i: (layer, i, 0))]
        + [whole(w) for w in weights] + [whole(g) for g in gains],
        out_specs=rows(D_MODEL),
        out_shape=jax.ShapeDtypeStruct((t, D_MODEL), F32),
        scratch_shapes=[pltpu.VMEM((POST_TM, D_MODEL), BF16)],
        compiler_params=pltpu.CompilerParams(
            dimension_semantics=("arbitrary",), vmem_limit_bytes=VMEM_LIMIT),
        name="post",
    )(h, att, ret, ga, gb, p, *weights, *gains)


def _rotary_tables(s, rot_dim, head_dim, theta):
    half = rot_dim // 2
    inv = theta ** (-jnp.arange(half, dtype=F32) * 2.0 / rot_dim)
    ang = jnp.arange(s).astype(F32)[:, None] * inv[None, :]
    within = jnp.arange(LANES) % head_dim
    cos = jnp.cos(ang)[:, within % half]
    sin = jnp.sin(ang)[:, within % half]
    rotated = within < rot_dim
    lower = within < half
    cos_t = jnp.where(rotated, cos, 1.0)
    sin_up = jnp.where(lower, -sin, 0.0)
    sin_dn = jnp.where(rotated & ~lower, sin, 0.0)
    return cos_t, sin_up, sin_dn


def kernel(x, p, w_in, w_att_out, w_ret_out, w_out, w_mlp_up, w_mlp_down, w_ple_gate, w_ple_proj,
           ret_decay_logit, norm_mix_pre, norm_mix_post, norm_mlp_pre, norm_mlp_post, norm_ple):
    b, s, d = x.shape
    depth = w_in.shape[0]
    t = b * s
    att_tables = _rotary_tables(s, ATT_ROPE_DIM, ATT_HEAD_DIM, ATT_ROPE_THETA)
    ret_tables = _rotary_tables(s, RET_QK_DIM, RET_QK_DIM, RET_THETA)

    def stacked(w):
        return w.astype(BF16)

    def gains(g):
        return g.reshape(depth, 1, d)

    w_in_b = stacked(w_in)
    post_weights = tuple(stacked(w) for w in (w_att_out, w_ret_out, w_out, w_mlp_up, w_mlp_down,
                                              w_ple_gate, w_ple_proj))
    post_gains = tuple(gains(g) for g in (norm_mix_post, norm_mlp_pre, norm_mlp_post, norm_ple))
    p_rows = p.reshape(depth, t, -1)

    h = x.reshape(t, d)
    for i in range(depth):
        qkv, rqk, rv, rg, ga, gb = _in_proj(i, h, gains(norm_mix_pre), w_in_b, att_tables, ret_tables, s)
        att = _attention(qkv.reshape(b, s, -1))
        decay_lanes = jnp.repeat(ret_decay_logit[i], RET_V_DIM, axis=-1)
        ret = _retention(rqk.reshape(b, s, -1), rv.reshape(b, s, -1), rg.reshape(b, s, -1),
                         decay_lanes)
        h = _post(i, h, att.reshape(t, -1), ret.reshape(t, -1), ga, gb, p_rows,
                  *post_weights, *post_gains)
    return h.reshape(b, s, d)
```

```python
import functools

import jax
import jax.numpy as jnp
from jax import lax
from jax.experimental import pallas as pl
from jax.experimental.pallas import tpu as pltpu

D_MODEL = 1024
ATT_HEADS = 8
ATT_HEAD_DIM = 64
ATT_ROPE_DIM = 16
ATT_ROPE_THETA = 500000.0
DILATIONS = (1, 4, 16)
HALF_WIDTH = 64
RET_HEADS = 8
RET_QK_DIM = 64
RET_V_DIM = 128
RET_THETA = 10000.0
RET_CHUNK = 128
D_FF = 4096
PLE_DIM = 256
NORM_EPS = 1e-6
GN_EPS = 1e-5
MASK_VALUE = -1e30

ATT_W = ATT_HEADS * ATT_HEAD_DIM
RET_QK_W = RET_HEADS * RET_QK_DIM
RET_V_W = RET_HEADS * RET_V_DIM
IN_WIDTH = 3 * ATT_W + 2 * RET_QK_W + 2 * RET_V_W + 2 * D_MODEL

LANES = 128
QBLK = 128
KBLK = QBLK + 2 * HALF_WIDTH
VMEM_LIMIT = 56 * 1024 * 1024

F32 = jnp.float32
BF16 = jnp.bfloat16


def _rms(x):
    return x * lax.rsqrt(jnp.mean(x * x, axis=-1, keepdims=True) + NORM_EPS)


def _sigmoid(x):
    return 1.0 / (1.0 + jnp.exp(-x))


def _rotate(x, cos, sin_up, sin_dn, shift):
    return (x * cos + pltpu.roll(x, LANES - shift, 1) * sin_up
            + pltpu.roll(x, shift, 1) * sin_dn)


IN_TM = 512
IN_TN = 512
_IN_OUT_WIDTHS = (3 * ATT_W, 2 * RET_QK_W, RET_V_W, RET_V_W, D_MODEL, D_MODEL)


def _in_proj_kernel(x_ref, g_ref, w_ref, ac_ref, au_ref, ad_ref, rc_ref, ru_ref, rd_ref, *rest):
    out_refs, u_ref = rest[:-1], rest[-1]
    u_ref[...] = (_rms(x_ref[...]) * g_ref[...]).astype(BF16)

    def rotary(tables, half):
        def apply(y):
            cos, sup, sdn = (tb[...] for tb in tables)
            return jnp.concatenate(
                [_rotate(y[:, j:j + LANES], cos, sup, sdn, half) for j in range(0, IN_TN, LANES)], axis=1)
        return apply

    att_rot = rotary((ac_ref, au_ref, ad_ref), ATT_ROPE_DIM // 2)
    ret_rot = rotary((rc_ref, ru_ref, rd_ref), RET_QK_DIM // 2)
    epilogues = {(0, 0): lambda y: att_rot(y) * ATT_Q_SCALE, (0, ATT_W): att_rot,
                 (1, 0): ret_rot, (1, RET_QK_W): lambda y: ret_rot(y) * (RET_QK_DIM ** -0.5),
                 (3, 0): lambda y: y * _sigmoid(y), (3, IN_TN): lambda y: y * _sigmoid(y)}
    off = 0
    for n, (o_ref, width) in enumerate(zip(out_refs, _IN_OUT_WIDTHS)):
        for c in range(0, width, IN_TN):
            y = jnp.dot(u_ref[...], w_ref[:, off + c:off + c + IN_TN], preferred_element_type=F32)
            o_ref[:, c:c + IN_TN] = epilogues.get((n, c), lambda y: y)(y).astype(BF16)
        off += width


def _in_proj(layer, h, gain, w, att_tables, ret_tables, s_len):
    t = h.shape[0]
    table = pl.BlockSpec((IN_TM, LANES), lambda i: (i % (s_len // IN_TM), 0))
    return pl.pallas_call(
        _in_proj_kernel,
        grid=(t // IN_TM,),
        in_specs=[
            pl.BlockSpec((IN_TM, D_MODEL), lambda i: (i, 0)),
            pl.BlockSpec((None, 1, D_MODEL), lambda i: (layer, 0, 0)),
            pl.BlockSpec((None, D_MODEL, IN_WIDTH), lambda i: (layer, 0, 0), pipeline_mode=pl.Buffered(1)),
        ] + [table] * 6,
        out_specs=[pl.BlockSpec((IN_TM, wd), lambda i: (i, 0)) for wd in _IN_OUT_WIDTHS],
        out_shape=[jax.ShapeDtypeStruct((t, wd), BF16) for wd in _IN_OUT_WIDTHS],
        scratch_shapes=[pltpu.VMEM((IN_TM, D_MODEL), BF16)],
        compiler_params=pltpu.CompilerParams(
            dimension_semantics=("arbitrary",), vmem_limit_bytes=VMEM_LIMIT),
        name="in_proj",
    )(h, gain, w, *att_tables, *ret_tables)


ATT_PAD = HALF_WIDTH * max(DILATIONS)
PREP_ROWS = 512
ATT_Q_SCALE = ATT_HEAD_DIM ** -0.5 * 1.4426950408889634
ATT_UNROLL = 8


def _attn_kernel(q_ref, k_ref, v_ref, o_ref, qs, kp, vp, acc, mrun, lrun, band):
    s_len = q_ref.shape[0]
    zeros = jnp.zeros((ATT_PAD, LANES), F32)
    kp[0:ATT_PAD, :] = zeros
    kp[ATT_PAD + s_len:, :] = zeros
    vp[0:ATT_PAD, :] = zeros
    vp[ATT_PAD + s_len:, :] = zeros

    d = (lax.broadcasted_iota(jnp.int32, (QBLK, KBLK), 1)
         - lax.broadcasted_iota(jnp.int32, (QBLK, KBLK), 0))
    for n, lo in enumerate((0, -HALF_WIDTH, QBLK - HALF_WIDTH)):
        band[n] = jnp.where((d >= lo) & (d <= lo + 2 * HALF_WIDTH), 0.0, MASK_VALUE).astype(F32)

    def prep(j, carry):
        r0 = pl.multiple_of(j * PREP_ROWS, PREP_ROWS)
        rows = pl.ds(r0, PREP_ROWS)
        qs[rows, :] = q_ref[rows, :].astype(F32)
        kp[pl.ds(ATT_PAD + r0, PREP_ROWS), :] = k_ref[rows, :].astype(F32)
        vp[pl.ds(ATT_PAD + r0, PREP_ROWS), :] = v_ref[rows, :].astype(F32)
        return carry

    lax.fori_loop(0, s_len // PREP_ROWS, prep, 0)

    lane_q = lax.broadcasted_iota(jnp.int32, (QBLK, LANES), 1)
    head_q = (lane_q < ATT_HEAD_DIM, lane_q >= ATT_HEAD_DIM)
    key_j = lax.broadcasted_iota(jnp.int32, (1, KBLK), 1)

    def rows_of(start, size, r):
        return pl.ds(start, size) if r == 1 else pl.ds(start, size, stride=r)

    def block(q_rows, kb, v2, bias, first, out_rows=None):
        qb = qs[q_rows, :]
        q2 = jnp.concatenate([jnp.where(head_q[0], qb, 0.0), jnp.where(head_q[1], qb, 0.0)],
                             axis=0).astype(BF16)
        s = lax.dot_general(q2, kb, (((1,), (1,)), ((), ())), preferred_element_type=F32)
        s = s + jnp.concatenate([bias, bias], axis=0)
        m = jnp.max(s, axis=-1, keepdims=True)
        p = jnp.exp2(s - m).astype(BF16)
        nd = jnp.dot(p, v2, preferred_element_type=F32)
        num = jnp.where(head_q[0], nd[:QBLK, :LANES], nd[QBLK:, :LANES])
        den = jnp.where(head_q[0], nd[:QBLK, LANES:], nd[QBLK:, LANES:])
        mx = jnp.where(head_q[0], m[:QBLK], m[QBLK:])
        if first:
            acc[q_rows, :] = num
            lrun[q_rows, :] = den
            mrun[q_rows, :] = mx
        else:
            m_old = mrun[q_rows, :]
            m_new = jnp.maximum(m_old, mx)
            a_old = jnp.exp2(m_old - m_new)
            a_new = jnp.exp2(mx - m_new)
            num = a_old * acc[q_rows, :] + a_new * num
            den = a_old * lrun[q_rows, :] + a_new * den
            if out_rows is not None:
                o_ref[out_rows, :] = (num / den).astype(o_ref.dtype)
            else:
                acc[q_rows, :] = num
                lrun[q_rows, :] = den
                mrun[q_rows, :] = m_new

    def load_keys(k_rows):
        vb = vp[k_rows, :].astype(BF16)
        return kp[k_rows, :].astype(BF16), jnp.concatenate([vb, jnp.ones_like(vb)], axis=1)

    def run_group(r, first, last):
        direct = last and not first and r == 1
        sub_len = s_len // r
        log2r = r.bit_length() - 1

        def body(t, carry):
            c = jnp.bitwise_and(t, r - 1)
            i = lax.shift_right_logical(t, log2r)
            q_start = c + (r * QBLK) * i
            kb, v2 = load_keys(rows_of(ATT_PAD - r * HALF_WIDTH + q_start, KBLK, r))
            key_pos = QBLK * i - HALF_WIDTH + key_j
            bias = band[0] + jnp.where((key_pos >= 0) & (key_pos < sub_len), 0.0, MASK_VALUE)
            out_rows = pl.ds(pl.multiple_of(q_start, QBLK), QBLK) if direct else None
            block(rows_of(q_start, QBLK, r), kb, v2, bias, first, out_rows)
            return carry

        def whole(c, carry):
            kb, v2 = load_keys(rows_of(ATT_PAD + c, KBLK, r))
            for i in range(KBLK // QBLK):
                block(rows_of(c + (r * QBLK) * i, QBLK, r), kb, v2, band[1 + i], first)
            return carry

        if sub_len == KBLK:
            lax.fori_loop(0, r, whole, 0, unroll=ATT_UNROLL // 2)
        else:
            lax.fori_loop(0, s_len // QBLK, body, 0, unroll=ATT_UNROLL)

    order = sorted(DILATIONS, reverse=True)
    for g, r in enumerate(order):
        run_group(r, g == 0, g == len(order) - 1)
    if len(order) > 1 and order[-1] == 1:
        return

    def finish(j, carry):
        rows = pl.ds(pl.multiple_of(j * PREP_ROWS, PREP_ROWS), PREP_ROWS)
        o_ref[rows, :] = (acc[rows, :] / lrun[rows, :]).astype(o_ref.dtype)
        return carry

    lax.fori_loop(0, s_len // PREP_ROWS, finish, 0)


def _attention(qkv):
    b, s, _ = qkv.shape
    pairs = ATT_W // LANES

    def col(section):
        return pl.BlockSpec((None, s, LANES), lambda bi, hp: (bi, 0, section * pairs + hp))

    return pl.pallas_call(
        _attn_kernel,
        grid=(b, pairs),
        in_specs=[col(0), col(1), col(2)],
        out_specs=pl.BlockSpec((None, s, LANES), lambda bi, hp: (bi, 0, hp)),
        out_shape=jax.ShapeDtypeStruct((b, s, ATT_W), BF16),
        scratch_shapes=[
            pltpu.VMEM((s, LANES), F32),
            pltpu.VMEM((s + 2 * ATT_PAD, LANES), F32),
            pltpu.VMEM((s + 2 * ATT_PAD, LANES), F32),
            pltpu.VMEM((s, LANES), F32),
            pltpu.VMEM((s, LANES), F32),
            pltpu.VMEM((s, LANES), F32),
            pltpu.VMEM((3, QBLK, KBLK), F32),
        ],
        compiler_params=pltpu.CompilerParams(
            dimension_semantics=("arbitrary", "arbitrary"), vmem_limit_bytes=VMEM_LIMIT),
        name="attention",
    )(qkv, qkv, qkv)


RET_UNROLL = 8


def _ret_kernel(q_ref, k_ref, v_ref, g_ref, dl_ref, o_ref, kt, states, sf, sb):
    s_len = q_ref.shape[0]
    c = RET_CHUNK
    n_chunks = s_len // c

    def prep(t, carry):
        rows = pl.ds(pl.multiple_of(t * c, c), c)
        kt[t] = k_ref[rows, :].astype(F32).T
        return carry

    lax.fori_loop(0, n_chunks, prep, 0, unroll=4)

    x = dl_ref[...]
    log_gamma = jnp.minimum(x, 0.0) - jnp.log(1.0 + jnp.exp(-jnp.abs(x)))
    lgf, lgb = log_gamma[0:1, :], log_gamma[1:2, :]
    row_i = lax.broadcasted_iota(jnp.int32, (c, LANES), 0)
    lane_i = lax.broadcasted_iota(jnp.int32, (c, LANES), 1)
    row, lane = row_i.astype(F32), lane_i.astype(F32)
    head_lane = (lane_i < RET_QK_DIM, lane_i >= RET_QK_DIM)

    def per_row(lg):
        return jnp.where(row_i < RET_QK_DIM, lg[:, :LANES], lg[:, LANES:])

    def per_lane(lg):
        return jnp.where(head_lane[0], lg[:, :LANES], lg[:, LANES:])

    zeta_f = jnp.exp((c - 1.0 - lane) * per_row(lgf))
    zeta_b = jnp.exp(lane * per_row(lgb))
    xi_f = jnp.exp((row + 1.0) * per_lane(lgf))
    xi_b = jnp.exp((c - row) * per_lane(lgb))
    diff = row - lane
    decay = jnp.concatenate(
        [jnp.exp(jnp.where(diff >= 0, diff * lgf[:, h * LANES:(h + 1) * LANES],
                           -diff * lgb[:, h * LANES:(h + 1) * LANES])) for h in range(2)], axis=0)

    srow = lax.broadcasted_iota(jnp.int32, (LANES, 2 * LANES), 0)
    scol = lax.broadcasted_iota(jnp.int32, (LANES, 2 * LANES), 1)
    same_head = (srow < RET_QK_DIM) == (scol < LANES)
    gc_f = jnp.where(same_head, jnp.exp(c * lgf), 0.0)
    gc_b = jnp.where(same_head, jnp.exp(c * lgb), 0.0)

    sf[...] = jnp.zeros_like(sf)
    sb[...] = jnp.zeros_like(sb)

    def scan(t, carry):
        for state, zeta, gc, tt, half_rows in ((sf, zeta_f, gc_f, t, slice(0, LANES)),
                                               (sb, zeta_b, gc_b, n_chunks - 1 - t, slice(LANES, 2 * LANES))):
            rows = pl.ds(pl.multiple_of(tt * c, c), c)
            old = state[...]
            states[tt, half_rows, :] = old.astype(BF16)
            kv = jnp.dot((kt[tt] * zeta).astype(BF16), v_ref[rows, :], preferred_element_type=F32)
            state[...] = gc * old + jnp.where(same_head, kv, 0.0)
        return carry

    lax.fori_loop(0, n_chunks, scan, 0, unroll=RET_UNROLL)

    vrow = lax.broadcasted_iota(jnp.int32, (2 * c, 2 * LANES), 0)
    vcol = lax.broadcasted_iota(jnp.int32, (2 * c, 2 * LANES), 1)
    v_diag = (vrow < c) == (vcol < LANES)

    def combine(t, carry):
        rows = pl.ds(pl.multiple_of(t * c, c), c)
        q = q_ref[rows, :].astype(F32)
        q2 = jnp.concatenate([jnp.where(head_lane[0], q, 0.0), jnp.where(head_lane[1], q, 0.0)],
                             axis=0).astype(BF16)
        s = jnp.dot(q2, kt[t].astype(BF16), preferred_element_type=F32) * decay
        s_wide = jnp.concatenate([s[:c], s[c:]], axis=1).astype(BF16)
        v = v_ref[rows, :]
        v_blocks = jnp.where(v_diag, jnp.concatenate([v, v], axis=0), 0.0)
        y = jnp.dot(s_wide, v_blocks, preferred_element_type=F32)
        q_cross = jnp.concatenate([q * xi_f, q * xi_b], axis=1).astype(BF16)
        y = y + jnp.dot(q_cross, states[t], preferred_element_type=F32)
        for hh in range(2):
            cols = slice(hh * LANES, (hh + 1) * LANES)
            yh = y[:, cols]
            mu = jnp.mean(yh, axis=-1, keepdims=True)
            yc = yh - mu
            var = jnp.mean(yc * yc, axis=-1, keepdims=True)
            yn = yc * lax.rsqrt(var + GN_EPS)
            o_ref[rows, cols] = (g_ref[rows, cols].astype(F32) * yn).astype(o_ref.dtype)
        return carry

    lax.fori_loop(0, n_chunks, combine, 0, unroll=RET_UNROLL)


def _retention(qk, v, g, decay_lanes):
    b, s, _ = qk.shape
    pairs = RET_QK_W // LANES
    n_chunks = s // RET_CHUNK
    wide = pl.BlockSpec((None, s, 2 * LANES), lambda bi, hp: (bi, 0, hp))
    return pl.pallas_call(
        _ret_kernel,
        grid=(b, pairs),
        in_specs=[
            pl.BlockSpec((None, s, LANES), lambda bi, hp: (bi, 0, hp)),
            pl.BlockSpec((None, s, LANES), lambda bi, hp: (bi, 0, pairs + hp)),
            wide, wide,
            pl.BlockSpec((2, 2 * LANES), lambda bi, hp: (0, hp)),
        ],
        out_specs=wide,
        out_shape=jax.ShapeDtypeStruct((b, s, RET_V_W), BF16),
        scratch_shapes=[
            pltpu.VMEM((n_chunks, LANES, RET_CHUNK), F32),
            pltpu.VMEM((n_chunks, 2 * LANES, 2 * LANES), BF16),
            pltpu.VMEM((LANES, 2 * LANES), F32),
            pltpu.VMEM((LANES, 2 * LANES), F32),
        ],
        compiler_params=pltpu.CompilerParams(
            dimension_semantics=("arbitrary", "arbitrary"), vmem_limit_bytes=VMEM_LIMIT),
        name="retention",
    )(qk, qk, v, g, decay_lanes)


POST_TM = 512
FF_CHUNK = 1024


def _post_kernel(h_ref, att_ref, ret_ref, ga_ref, gb_ref, p_ref,
                 w_att_ref, w_ret_ref, w_out_ref, w_up_ref, w_down_ref, w_gate_ref, w_proj_ref,
                 n_post_ref, n_mlp_pre_ref, n_mlp_post_ref, n_ple_ref, o_ref, u_ref):
    a = jnp.dot(att_ref[...], w_att_ref[...], preferred_element_type=F32)
    r = jnp.dot(ret_ref[...], w_ret_ref[...], preferred_element_type=F32)
    merged = (_sigmoid(ga_ref[...].astype(F32)) * a + _sigmoid(gb_ref[...].astype(F32)) * r)
    y = jnp.dot(merged.astype(BF16), w_out_ref[...], preferred_element_type=F32)
    h = h_ref[...] + _rms(y) * n_post_ref[...]

    u_ref[...] = (_rms(h) * n_mlp_pre_ref[...]).astype(BF16)
    ff = jnp.zeros((POST_TM, D_MODEL), F32)
    for c in range(0, D_FF, FF_CHUNK):
        z = jnp.dot(u_ref[...], w_up_ref[:, c:c + FF_CHUNK], preferred_element_type=F32)
        z = jnp.maximum(z, 0.0)
        ff = ff + jnp.dot((z * z).astype(BF16), w_down_ref[c:c + FF_CHUNK, :],
                          preferred_element_type=F32)
    h = h + _rms(ff) * n_mlp_post_ref[...]

    gate = _sigmoid(jnp.dot(_rms(h).astype(BF16), w_gate_ref[...], preferred_element_type=F32))
    pe = jnp.dot(p_ref[...].astype(BF16), w_proj_ref[...], preferred_element_type=F32)
    o_ref[...] = h + gate * (_rms(pe) * n_ple_ref[...])


def _post(layer, h, att, ret, ga, gb, p, w_att, w_ret, w_out, w_up, w_down, w_gate, w_proj,
          n_post, n_mlp_pre, n_mlp_post, n_ple):
    t = h.shape[0]

    def rows(width):
        return pl.BlockSpec((POST_TM, width), lambda i: (i, 0))

    def whole(arr):
        return pl.BlockSpec((None,) + arr.shape[1:], lambda i: (layer, 0, 0), pipeline_mode=pl.Buffered(1))

    weights = (w_att, w_ret, w_out, w_up, w_down, w_gate, w_proj)
    gains = (n_post, n_mlp_pre, n_mlp_post, n_ple)
    return pl.pallas_call(
        _post_kernel,
        grid=(t // POST_TM,),
        in_specs=[rows(D_MODEL), rows(ATT_W), rows(RET_V_W), rows(D_MODEL), rows(D_MODEL),
                  pl.BlockSpec((None, POST_TM, PLE_DIM), lambda i: (layer, i, 0))]
        + [whole(w) for w in weights] + [whole(g) for g in gains],
        out_specs=rows(D_MODEL),
        out_shape=jax.ShapeDtypeStruct((t, D_MODEL), F32),
        scratch_shapes=[pltpu.VMEM((POST_TM, D_MODEL), BF16)],
        compiler_params=pltpu.CompilerParams(
            dimension_semantics=("arbitrary",), vmem_limit_bytes=VMEM_LIMIT),
        name="post",
    )(h, att, ret, ga, gb, p, *weights, *gains)


def _rotary_tables(s, rot_dim, head_dim, theta):
    half = rot_dim // 2
    inv = theta ** (-jnp.arange(half, dtype=F32) * 2.0 / rot_dim)
    ang = jnp.arange(s).astype(F32)[:, None] * inv[None, :]
    within = jnp.arange(LANES) % head_dim
    cos = jnp.cos(ang)[:, within % half]
    sin = jnp.sin(ang)[:, within % half]
    rotated = within < rot_dim
    lower = within < half
    cos_t = jnp.where(rotated, cos, 1.0)
    sin_up = jnp.where(lower, -sin, 0.0)
    sin_dn = jnp.where(rotated & ~lower, sin, 0.0)
    return cos_t, sin_up, sin_dn


def kernel(x, p, w_in, w_att_out, w_ret_out, w_out, w_mlp_up, w_mlp_down, w_ple_gate, w_ple_proj,
           ret_decay_logit, norm_mix_pre, norm_mix_post, norm_mlp_pre, norm_mlp_post, norm_ple):
    b, s, d = x.shape
    depth = w_in.shape[0]
    t = b * s
    att_tables = _rotary_tables(s, ATT_ROPE_DIM, ATT_HEAD_DIM, ATT_ROPE_THETA)
    ret_tables = _rotary_tables(s, RET_QK_DIM, RET_QK_DIM, RET_THETA)

    def stacked(w):
        return w.astype(BF16)

    def gains(g):
        return g.reshape(depth, 1, d)

    w_in_b = stacked(w_in)
    post_weights = tuple(stacked(w) for w in (w_att_out, w_ret_out, w_out, w_mlp_up, w_mlp_down,
                                              w_ple_gate, w_ple_proj))
    post_gains = tuple(gains(g) for g in (norm_mix_post, norm_mlp_pre, norm_mlp_post, norm_ple))
    p_rows = p.reshape(depth, t, -1)

    h = x.reshape(t, d)
    for i in range(depth):
        qkv, rqk, rv, rg, ga, gb = _in_proj(i, h, gains(norm_mix_pre), w_in_b, att_tables, ret_tables, s)
        att = _attention(qkv.reshape(b, s, -1))
        decay_lanes = jnp.repeat(ret_decay_logit[i], RET_V_DIM, axis=-1)
        ret = _retention(rqk.reshape(b, s, -1), rv.reshape(b, s, -1), rg.reshape(b, s, -1),
                         decay_lanes)
        h = _post(i, h, att.reshape(t, -1), ret.reshape(t, -1), ga, gb, p_rows,
                  *post_weights, *post_gains)
    return h.reshape(b, s, d)
```

```python
import functools

import jax
import jax.numpy as jnp
from jax import lax
from jax.experimental import pallas as pl
from jax.experimental.pallas import tpu as pltpu

D_MODEL = 1024
ATT_HEADS = 8
ATT_HEAD_DIM = 64
ATT_ROPE_DIM = 16
ATT_ROPE_THETA = 500000.0
DILATIONS = (1, 4, 16)
HALF_WIDTH = 64
RET_HEADS = 8
RET_QK_DIM = 64
RET_V_DIM = 128
RET_THETA = 10000.0
RET_CHUNK = 128
D_FF = 4096
PLE_DIM = 256
NORM_EPS = 1e-6
GN_EPS = 1e-5
MASK_VALUE = -1e30

ATT_W = ATT_HEADS * ATT_HEAD_DIM
RET_QK_W = RET_HEADS * RET_QK_DIM
RET_V_W = RET_HEADS * RET_V_DIM
IN_WIDTH = 3 * ATT_W + 2 * RET_QK_W + 2 * RET_V_W + 2 * D_MODEL

LANES = 128
QBLK = 128
KBLK = QBLK + 2 * HALF_WIDTH
VMEM_LIMIT = 56 * 1024 * 1024

F32 = jnp.float32
BF16 = jnp.bfloat16


def _rms(x):
    return x * lax.rsqrt(jnp.mean(x * x, axis=-1, keepdims=True) + NORM_EPS)


def _sigmoid(x):
    return 1.0 / (1.0 + jnp.exp(-x))


def _rotate(x, cos, sin_up, sin_dn, shift):
    return (x * cos + pltpu.roll(x, LANES - shift, 1) * sin_up
            + pltpu.roll(x, shift, 1) * sin_dn)


IN_TM = 512
IN_TN = 512
_IN_OUT_WIDTHS = (3 * ATT_W, 2 * RET_QK_W, RET_V_W, RET_V_W, D_MODEL, D_MODEL)


def _in_proj_kernel(x_ref, g_ref, w_ref, ac_ref, au_ref, ad_ref, rc_ref, ru_ref, rd_ref, *rest):
    out_refs, u_ref = rest[:-1], rest[-1]
    u_ref[...] = (_rms(x_ref[...]) * g_ref[...]).astype(BF16)

    def rotary(tables, half):
        def apply(y):
            cos, sup, sdn = (tb[...] for tb in tables)
            return jnp.concatenate(
                [_rotate(y[:, j:j + LANES], cos, sup, sdn, half) for j in range(0, IN_TN, LANES)], axis=1)
        return apply

    att_rot = rotary((ac_ref, au_ref, ad_ref), ATT_ROPE_DIM // 2)
    ret_rot = rotary((rc_ref, ru_ref, rd_ref), RET_QK_DIM // 2)
    epilogues = {(0, 0): lambda y: att_rot(y) * ATT_Q_SCALE, (0, ATT_W): att_rot,
                 (1, 0): ret_rot, (1, RET_QK_W): lambda y: ret_rot(y) * (RET_QK_DIM ** -0.5),
                 (3, 0): lambda y: y * _sigmoid(y), (3, IN_TN): lambda y: y * _sigmoid(y)}
    off = 0
    for n, (o_ref, width) in enumerate(zip(out_refs, _IN_OUT_WIDTHS)):
        for c in range(0, width, IN_TN):
            y = jnp.dot(u_ref[...], w_ref[:, off + c:off + c + IN_TN], preferred_element_type=F32)
            o_ref[:, c:c + IN_TN] = epilogues.get((n, c), lambda y: y)(y).astype(BF16)
        off += width


def _in_proj(layer, h, gain, w, att_tables, ret_tables, s_len):
    t = h.shape[0]
    table = pl.BlockSpec((IN_TM, LANES), lambda i: (i % (s_len // IN_TM), 0))
    return pl.pallas_call(
        _in_proj_kernel,
        grid=(t // IN_TM,),
        in_specs=[
            pl.BlockSpec((IN_TM, D_MODEL), lambda i: (i, 0)),
            pl.BlockSpec((None, 1, D_MODEL), lambda i: (layer, 0, 0)),
            pl.BlockSpec((None, D_MODEL, IN_WIDTH), lambda i: (layer, 0, 0), pipeline_mode=pl.Buffered(1)),
        ] + [table] * 6,
        out_specs=[pl.BlockSpec((IN_TM, wd), lambda i: (i, 0)) for wd in _IN_OUT_WIDTHS],
        out_shape=[jax.ShapeDtypeStruct((t, wd), BF16) for wd in _IN_OUT_WIDTHS],
        scratch_shapes=[pltpu.VMEM((IN_TM, D_MODEL), BF16)],
        compiler_params=pltpu.CompilerParams(
            dimension_semantics=("arbitrary",), vmem_limit_bytes=VMEM_LIMIT),
        name="in_proj",
    )(h, gain, w, *att_tables, *ret_tables)


ATT_PAD = HALF_WIDTH * max(DILATIONS)
PREP_ROWS = 512
ATT_Q_SCALE = ATT_HEAD_DIM ** -0.5 * 1.4426950408889634
ATT_UNROLL = 8


def _attn_kernel(q_ref, k_ref, v_ref, o_ref, qs, kp, vp, acc, mrun, lrun, band):
    s_len = q_ref.shape[0]
    zeros = jnp.zeros((ATT_PAD, LANES), F32)
    kp[0:ATT_PAD, :] = zeros
    kp[ATT_PAD + s_len:, :] = zeros
    vp[0:ATT_PAD, :] = zeros
    vp[ATT_PAD + s_len:, :] = zeros

    d = (lax.broadcasted_iota(jnp.int32, (QBLK, KBLK), 1)
         - lax.broadcasted_iota(jnp.int32, (QBLK, KBLK), 0))
    for n, lo in enumerate((0, -HALF_WIDTH, QBLK - HALF_WIDTH)):
        band[n] = jnp.where((d >= lo) & (d <= lo + 2 * HALF_WIDTH), 0.0, MASK_VALUE).astype(F32)

    def prep(j, carry):
        r0 = pl.multiple_of(j * PREP_ROWS, PREP_ROWS)
        rows = pl.ds(r0, PREP_ROWS)
        qs[rows, :] = q_ref[rows, :].astype(F32)
        kp[pl.ds(ATT_PAD + r0, PREP_ROWS), :] = k_ref[rows, :].astype(F32)
        vp[pl.ds(ATT_PAD + r0, PREP_ROWS), :] = v_ref[rows, :].astype(F32)
        return carry

    lax.fori_loop(0, s_len // PREP_ROWS, prep, 0)

    lane_q = lax.broadcasted_iota(jnp.int32, (QBLK, LANES), 1)
    head_q = (lane_q < ATT_HEAD_DIM, lane_q >= ATT_HEAD_DIM)
    key_j = lax.broadcasted_iota(jnp.int32, (1, KBLK), 1)

    def rows_of(start, size, r):
        return pl.ds(start, size) if r == 1 else pl.ds(start, size, stride=r)

    def block(q_rows, kb, v2, bias, first, out_rows=None):
        qb = qs[q_rows, :]
        q2 = jnp.concatenate([jnp.where(head_q[0], qb, 0.0), jnp.where(head_q[1], qb, 0.0)],
                             axis=0).astype(BF16)
        s = lax.dot_general(q2, kb, (((1,), (1,)), ((), ())), preferred_element_type=F32)
        s = s + jnp.concatenate([bias, bias], axis=0)
        m = jnp.max(s, axis=-1, keepdims=True)
        p = jnp.exp2(s - m).astype(BF16)
        nd = jnp.dot(p, v2, preferred_element_type=F32)
        num = jnp.where(head_q[0], nd[:QBLK, :LANES], nd[QBLK:, :LANES])
        den = jnp.where(head_q[0], nd[:QBLK, LANES:], nd[QBLK:, LANES:])
        mx = jnp.where(head_q[0], m[:QBLK], m[QBLK:])
        if first:
            acc[q_rows, :] = num
            lrun[q_rows, :] = den
            mrun[q_rows, :] = mx
        else:
            m_old = mrun[q_rows, :]
            m_new = jnp.maximum(m_old, mx)
            a_old = jnp.exp2(m_old - m_new)
            a_new = jnp.exp2(mx - m_new)
            num = a_old * acc[q_rows, :] + a_new * num
            den = a_old * lrun[q_rows, :] + a_new * den
            if out_rows is not None:
                o_ref[out_rows, :] = (num / den).astype(o_ref.dtype)
            else:
                acc[q_rows, :] = num
                lrun[q_rows, :] = den
                mrun[q_rows, :] = m_new

    def load_keys(k_rows):
        vb = vp[k_rows, :].astype(BF16)
        return kp[k_rows, :].astype(BF16), jnp.concatenate([vb, jnp.ones_like(vb)], axis=1)

    def run_group(r, first, last):
        direct = last and not first and r == 1
        sub_len = s_len // r
        log2r = r.bit_length() - 1

        def body(t, carry):
            c = jnp.bitwise_and(t, r - 1)
            i = lax.shift_right_logical(t, log2r)
            q_start = c + (r * QBLK) * i
            kb, v2 = load_keys(rows_of(ATT_PAD - r * HALF_WIDTH + q_start, KBLK, r))
            key_pos = QBLK * i - HALF_WIDTH + key_j
            bias = band[0] + jnp.where((key_pos >= 0) & (key_pos < sub_len), 0.0, MASK_VALUE)
            out_rows = pl.ds(pl.multiple_of(q_start, QBLK), QBLK) if direct else None
            block(rows_of(q_start, QBLK, r), kb, v2, bias, first, out_rows)
            return carry

        def whole(c, carry):
            kb, v2 = load_keys(rows_of(ATT_PAD + c, KBLK, r))
            for i in range(KBLK // QBLK):
                block(rows_of(c + (r * QBLK) * i, QBLK, r), kb, v2, band[1 + i], first)
            return carry

        if sub_len == KBLK:
            lax.fori_loop(0, r, whole, 0, unroll=ATT_UNROLL // 2)
        else:
            lax.fori_loop(0, s_len // QBLK, body, 0, unroll=ATT_UNROLL)

    order = sorted(DILATIONS, reverse=True)
    for g, r in enumerate(order):
        run_group(r, g == 0, g == len(order) - 1)
    if len(order) > 1 and order[-1] == 1:
        return

    def finish(j, carry):
        rows = pl.ds(pl.multiple_of(j * PREP_ROWS, PREP_ROWS), PREP_ROWS)
        o_ref[rows, :] = (acc[rows, :] / lrun[rows, :]).astype(o_ref.dtype)
        return carry

    lax.fori_loop(0, s_len // PREP_ROWS, finish, 0)


def _attention(qkv):
    b, s, _ = qkv.shape
    pairs = ATT_W // LANES

    def col(section):
        return pl.BlockSpec((None, s, LANES), lambda bi, hp: (bi, 0, section * pairs + hp))

    return pl.pallas_call(
        _attn_kernel,
        grid=(b, pairs),
        in_specs=[col(0), col(1), col(2)],
        out_specs=pl.BlockSpec((None, s, LANES), lambda bi, hp: (bi, 0, hp)),
        out_shape=jax.ShapeDtypeStruct((b, s, ATT_W), BF16),
        scratch_shapes=[
            pltpu.VMEM((s, LANES), F32),
            pltpu.VMEM((s + 2 * ATT_PAD, LANES), F32),
            pltpu.VMEM((s + 2 * ATT_PAD, LANES), F32),
            pltpu.VMEM((s, LANES), F32),
            pltpu.VMEM((s, LANES), F32),
            pltpu.VMEM((s, LANES), F32),
            pltpu.VMEM((3, QBLK, KBLK), F32),
        ],
        compiler_params=pltpu.CompilerParams(
            dimension_semantics=("arbitrary", "arbitrary"), vmem_limit_bytes=VMEM_LIMIT),
        name="attention",
    )(qkv, qkv, qkv)


RET_UNROLL = 16


def _ret_kernel(q_ref, k_ref, v_ref, g_ref, dl_ref, o_ref, kt, states, sf, sb):
    s_len = q_ref.shape[0]
    c = RET_CHUNK
    n_chunks = s_len // c

    def prep(t, carry):
        rows = pl.ds(pl.multiple_of(t * c, c), c)
        kt[t] = k_ref[rows, :].astype(F32).T
        return carry

    lax.fori_loop(0, n_chunks, prep, 0, unroll=4)

    x = dl_ref[...]
    log_gamma = jnp.minimum(x, 0.0) - jnp.log(1.0 + jnp.exp(-jnp.abs(x)))
    lgf, lgb = log_gamma[0:1, :], log_gamma[1:2, :]
    row_i = lax.broadcasted_iota(jnp.int32, (c, LANES), 0)
    lane_i = lax.broadcasted_iota(jnp.int32, (c, LANES), 1)
    row, lane = row_i.astype(F32), lane_i.astype(F32)
    head_lane = (lane_i < RET_QK_DIM, lane_i >= RET_QK_DIM)

    def per_row(lg):
        return jnp.where(row_i < RET_QK_DIM, lg[:, :LANES], lg[:, LANES:])

    def per_lane(lg):
        return jnp.where(head_lane[0], lg[:, :LANES], lg[:, LANES:])

    zeta_f = jnp.exp((c - 1.0 - lane) * per_row(lgf))
    zeta_b = jnp.exp(lane * per_row(lgb))
    xi_f = jnp.exp((row + 1.0) * per_lane(lgf))
    xi_b = jnp.exp((c - row) * per_lane(lgb))
    diff = row - lane
    decay = jnp.concatenate(
        [jnp.exp(jnp.where(diff >= 0, diff * lgf[:, h * LANES:(h + 1) * LANES],
                           -diff * lgb[:, h * LANES:(h + 1) * LANES])) for h in range(2)], axis=0)

    srow = lax.broadcasted_iota(jnp.int32, (LANES, 2 * LANES), 0)
    scol = lax.broadcasted_iota(jnp.int32, (LANES, 2 * LANES), 1)
    same_head = (srow < RET_QK_DIM) == (scol < LANES)
    gc_f = jnp.where(same_head, jnp.exp(c * lgf), 0.0)
    gc_b = jnp.where(same_head, jnp.exp(c * lgb), 0.0)

    sf[...] = jnp.zeros_like(sf)
    sb[...] = jnp.zeros_like(sb)

    def scan(t, carry):
        for state, zeta, gc, tt, half_rows in ((sf, zeta_f, gc_f, t, slice(0, LANES)),
                                               (sb, zeta_b, gc_b, n_chunks - 1 - t, slice(LANES, 2 * LANES))):
            rows = pl.ds(pl.multiple_of(tt * c, c), c)
            old = state[...]
            states[tt, half_rows, :] = old.astype(BF16)
            kv = jnp.dot((kt[tt] * zeta).astype(BF16), v_ref[rows, :], preferred_element_type=F32)
            state[...] = gc * old + jnp.where(same_head, kv, 0.0)
        return carry

    lax.fori_loop(0, n_chunks, scan, 0, unroll=RET_UNROLL)

    vrow = lax.broadcasted_iota(jnp.int32, (2 * c, 2 * LANES), 0)
    vcol = lax.broadcasted_iota(jnp.int32, (2 * c, 2 * LANES), 1)
    v_diag = (vrow < c) == (vcol < LANES)

    def combine(t, carry):
        rows = pl.ds(pl.multiple_of(t * c, c), c)
        q = q_ref[rows, :].astype(F32)
        q2 = jnp.concatenate([jnp.where(head_lane[0], q, 0.0), jnp.where(head_lane[1], q, 0.0)],
                             axis=0).astype(BF16)
        s = jnp.dot(q2, kt[t].astype(BF16), preferred_element_type=F32) * decay
        s_wide = jnp.concatenate([s[:c], s[c:]], axis=1).astype(BF16)
        v = v_ref[rows, :]
        v_blocks = jnp.where(v_diag, jnp.concatenate([v, v], axis=0), 0.0)
        y = jnp.dot(s_wide, v_blocks, preferred_element_type=F32)
        q_cross = jnp.concatenate([q * xi_f, q * xi_b], axis=1).astype(BF16)
        y = y + jnp.dot(q_cross, states[t], preferred_element_type=F32)
        for hh in range(2):
            cols = slice(hh * LANES, (hh + 1) * LANES)
            yh = y[:, cols]
            mu = jnp.mean(yh, axis=-1, keepdims=True)
            yc = yh - mu
            var = jnp.mean(yc * yc, axis=-1, keepdims=True)
            yn = yc * lax.rsqrt(var + GN_EPS)
            o_ref[rows, cols] = (g_ref[rows, cols].astype(F32) * yn).astype(o_ref.dtype)
        return carry

    lax.fori_loop(0, n_chunks, combine, 0, unroll=RET_UNROLL)


def _retention(qk, v, g, decay_lanes):
    b, s, _ = qk.shape
    pairs = RET_QK_W // LANES
    n_chunks = s // RET_CHUNK
    wide = pl.BlockSpec((None, s, 2 * LANES), lambda bi, hp: (bi, 0, hp))
    return pl.pallas_call(
        _ret_kernel,
        grid=(b, pairs),
        in_specs=[
            pl.BlockSpec((None, s, LANES), lambda bi, hp: (bi, 0, hp)),
            pl.BlockSpec((None, s, LANES), lambda bi, hp: (bi, 0, pairs + hp)),
            wide, wide,
            pl.BlockSpec((2, 2 * LANES), lambda bi, hp: (0, hp)),
        ],
        out_specs=wide,
        out_shape=jax.ShapeDtypeStruct((b, s, RET_V_W), BF16),
        scratch_shapes=[
            pltpu.VMEM((n_chunks, LANES, RET_CHUNK), F32),
            pltpu.VMEM((n_chunks, 2 * LANES, 2 * LANES), BF16),
            pltpu.VMEM((LANES, 2 * LANES), F32),
            pltpu.VMEM((LANES, 2 * LANES), F32),
        ],
        compiler_params=pltpu.CompilerParams(
            dimension_semantics=("arbitrary", "arbitrary"), vmem_limit_bytes=VMEM_LIMIT),
        name="retention",
    )(qk, qk, v, g, decay_lanes)


POST_TM = 512
FF_CHUNK = 1024


def _post_kernel(h_ref, att_ref, ret_ref, ga_ref, gb_ref, p_ref,
                 w_att_ref, w_ret_ref, w_out_ref, w_up_ref, w_down_ref, w_gate_ref, w_proj_ref,
                 n_post_ref, n_mlp_pre_ref, n_mlp_post_ref, n_ple_ref, o_ref, u_ref):
    a = jnp.dot(att_ref[...], w_att_ref[...], preferred_element_type=F32)
    r = jnp.dot(ret_ref[...], w_ret_ref[...], preferred_element_type=F32)
    merged = (_sigmoid(ga_ref[...].astype(F32)) * a + _sigmoid(gb_ref[...].astype(F32)) * r)
    y = jnp.dot(merged.astype(BF16), w_out_ref[...], preferred_element_type=F32)
    h = h_ref[...] + _rms(y) * n_post_ref[...]

    u_ref[...] = (_rms(h) * n_mlp_pre_ref[...]).astype(BF16)
    ff = jnp.zeros((POST_TM, D_MODEL), F32)
    for c in range(0, D_FF, FF_CHUNK):
        z = jnp.dot(u_ref[...], w_up_ref[:, c:c + FF_CHUNK], preferred_element_type=F32)
        z = jnp.maximum(z, 0.0)
        ff = ff + jnp.dot((z * z).astype(BF16), w_down_ref[c:c + FF_CHUNK, :],
                          preferred_element_type=F32)
    h = h + _rms(ff) * n_mlp_post_ref[...]

    gate = _sigmoid(jnp.dot(_rms(h).astype(BF16), w_gate_ref[...], preferred_element_type=F32))
    pe = jnp.dot(p_ref[...].astype(BF16), w_proj_ref[...], preferred_element_type=F32)
    o_ref[...] = h + gate * (_rms(pe) * n_ple_ref[...])


def _post(layer, h, att, ret, ga, gb, p, w_att, w_ret, w_out, w_up, w_down, w_gate, w_proj,
          n_post, n_mlp_pre, n_mlp_post, n_ple):
    t = h.shape[0]

    def rows(width):
        return pl.BlockSpec((POST_TM, width), lambda i: (i, 0))

    def whole(arr):
        return pl.BlockSpec((None,) + arr.shape[1:], lambda i: (layer, 0, 0), pipeline_mode=pl.Buffered(1))

    weights = (w_att, w_ret, w_out, w_up, w_down, w_gate, w_proj)
    gains = (n_post, n_mlp_pre, n_mlp_post, n_ple)
    return pl.pallas_call(
        _post_kernel,
        grid=(t // POST_TM,),
        in_specs=[rows(D_MODEL), rows(ATT_W), rows(RET_V_W), rows(D_MODEL), rows(D_MODEL),
                  pl.BlockSpec((None, POST_TM, PLE_DIM), lambda i: (layer, i, 0))]
        + [whole(w) for w in weights] + [whole(g) for g in gains],
        out_specs=rows(D_MODEL),
        out_shape=jax.ShapeDtypeStruct((t, D_MODEL), F32),
        scratch_shapes=[pltpu.VMEM((POST_TM, D_MODEL), BF16)],
        compiler_params=pltpu.CompilerParams(
            dimension_semantics=("arbitrary",), vmem_limit_bytes=VMEM_LIMIT),
        name="post",
    )(h, att, ret, ga, gb, p, *weights, *gains)


def _rotary_tables(s, rot_dim, head_dim, theta):
    half = rot_dim // 2
    inv = theta ** (-jnp.arange(half, dtype=F32) * 2.0 / rot_dim)
    ang = jnp.arange(s).astype(F32)[:, None] * inv[None, :]
    within = jnp.arange(LANES) % head_dim
    cos = jnp.cos(ang)[:, within % half]
    sin = jnp.sin(ang)[:, within % half]
    rotated = within < rot_dim
    lower = within < half
    cos_t = jnp.where(rotated, cos, 1.0)
    sin_up = jnp.where(lower, -sin, 0.0)
    sin_dn = jnp.where(rotated & ~lower, sin, 0.0)
    return cos_t, sin_up, sin_dn


def kernel(x, p, w_in, w_att_out, w_ret_out, w_out, w_mlp_up, w_mlp_down, w_ple_gate, w_ple_proj,
           ret_decay_logit, norm_mix_pre, norm_mix_post, norm_mlp_pre, norm_mlp_post, norm_ple):
    b, s, d = x.shape
    depth = w_in.shape[0]
    t = b * s
    att_tables = _rotary_tables(s, ATT_ROPE_DIM, ATT_HEAD_DIM, ATT_ROPE_THETA)
    ret_tables = _rotary_tables(s, RET_QK_DIM, RET_QK_DIM, RET_THETA)

    def stacked(w):
        return w.astype(BF16)

    def gains(g):
        return g.reshape(depth, 1, d)

    w_in_b = stacked(w_in)
    post_weights = tuple(stacked(w) for w in (w_att_out, w_ret_out, w_out, w_mlp_up, w_mlp_down,
                                              w_ple_gate, w_ple_proj))
    post_gains = tuple(gains(g) for g in (norm_mix_post, norm_mlp_pre, norm_mlp_post, norm_ple))
    p_rows = p.reshape(depth, t, -1)

    h = x.reshape(t, d)
    for i in range(depth):
        qkv, rqk, rv, rg, ga, gb = _in_proj(i, h, gains(norm_mix_pre), w_in_b, att_tables, ret_tables, s)
        att = _attention(qkv.reshape(b, s, -1))
        decay_lanes = jnp.repeat(ret_decay_logit[i], RET_V_DIM, axis=-1)
        ret = _retention(rqk.reshape(b, s, -1), rv.reshape(b, s, -1), rg.reshape(b, s, -1),
                         decay_lanes)
        h = _post(i, h, att.reshape(t, -1), ret.reshape(t, -1), ga, gb, p_rows,
                  *post_weights, *post_gains)
    return h.reshape(b, s, d)
```
